```python
import jax
import jax.numpy as jnp
from jax import lax
import numpy as np

D_MODEL = 1024
BATCH = 8
SEQ = 2048
DEPTH = 2

GRID_W = 64
CTX_LEN = 256

DN_HEADS = 4
DN_DK = 64
DN_DV = 64
DN_CONV = 3
DN_CHUNK = 64
GQA_HEADS = 8
GQA_KV_HEADS = 2
GQA_DIM = 64
MLA_HEADS = 4
MLA_Q_RANK = 192
MLA_KV_RANK = 128
MLA_NOPE = 64
MLA_ROPE = 32
MLA_V = 64
N_EXPERTS = 32
TOP_K = 4
D_EXPERT = 1024
SWIGLU_LIMIT = 7.0
SWIGLU_ALPHA = 1.702

Q_BLOCK = 128
ROPE_THETA = 10000.0
EPS = 1e-6
ALPHA = (2 * DEPTH) ** 0.25
BETA_INIT = (8 * DEPTH) ** -0.25

DN_QK_W = DN_HEADS * DN_DK
DN_V_W = DN_HEADS * DN_DV
IN_SIZES = (
    2 * DN_QK_W + DN_V_W,
    DN_V_W,
    4 * DN_HEADS,
    GQA_HEADS * GQA_DIM,
    GQA_KV_HEADS * GQA_DIM,
    GQA_KV_HEADS * GQA_DIM,
    MLA_Q_RANK,
    MLA_KV_RANK,
    MLA_ROPE,
)
IN_WIDTH = sum(IN_SIZES)
MIX_W = DN_V_W + GQA_HEADS * GQA_DIM + MLA_HEADS * MLA_V

kernel_name = "hybrid_parallel_heads_dit_moe"


def layer_norm(x, g=None, b=None):
    xf = x.astype(jnp.float32)
    mu = jnp.mean(xf, axis=-1, keepdims=True)
    var = jnp.mean(jnp.square(xf - mu), axis=-1, keepdims=True)
    y = (xf - mu) * lax.rsqrt(var + EPS)
    if g is not None:
        y = y * g.astype(jnp.float32) + b.astype(jnp.float32)
    return y.astype(x.dtype)


def rms_norm(x, w):
    xf = x.astype(jnp.float32)
    y = xf * lax.rsqrt(jnp.mean(jnp.square(xf), axis=-1, keepdims=True) + EPS)
    return (y * w.astype(jnp.float32)).astype(x.dtype)


def l2_normalize(x):
    return x * lax.rsqrt(jnp.sum(jnp.square(x), axis=-1, keepdims=True) + EPS)


def rope_1d(x, pos):
    half = x.shape[-1] // 2
    inv_freq = ROPE_THETA ** (-jnp.arange(half, dtype=jnp.float32) / half)
    ang = pos.astype(jnp.float32)[:, None] * inv_freq[None, :]
    cos, sin = jnp.cos(ang), jnp.sin(ang)
    x1 = x[..., :half].astype(jnp.float32)
    x2 = x[..., half:].astype(jnp.float32)
    return jnp.concatenate([x1 * cos - x2 * sin, x2 * cos + x1 * sin], axis=-1).astype(x.dtype)


def rope_2d(x, row, col):
    h = x.shape[-1] // 2
    return jnp.concatenate([rope_1d(x[..., :h], row), rope_1d(x[..., h:], col)], axis=-1)


def merge_heads(t):
    b, h, length, d = t.shape
    return jnp.transpose(t, (0, 2, 1, 3)).reshape(b, length, h * d)


def split_columns(z):
    offsets = np.cumsum(IN_SIZES)[:-1].tolist()
    return jnp.split(z, offsets, axis=-1)


def block_attention(q, k, v, scale):
    b, hkv, grp, lq, dk = q.shape
    nb = lq // Q_BLOCK
    qb = jnp.moveaxis(q.reshape(b, hkv, grp, nb, Q_BLOCK, dk), 3, 0)

    def one_block(qblk):
        s = jnp.einsum('bhgqd,bhkd->bhgqk', qblk, k).astype(jnp.float32) * scale
        p = jax.nn.softmax(s, axis=-1).astype(v.dtype)
        return jnp.einsum('bhgqk,bhkd->bhgqd', p, v)

    o = lax.map(one_block, qb)
    return jnp.moveaxis(o, 0, 3).reshape(b, hkv, grp, lq, v.shape[-1])


def short_conv(x, w):
    ch = x.shape[-1]
    y = lax.conv_general_dilated(
        x, w[:, None, :].astype(x.dtype), window_strides=(1,),
        padding=[((DN_CONV - 1) // 2, DN_CONV // 2)],
        dimension_numbers=('NWC', 'WIO', 'NWC'), feature_group_count=ch)
    return jax.nn.silu(y)


def chunk_gated_delta(q, k, v, g, beta, s0):
    b, h, length, dk = q.shape
    dv = v.shape[-1]
    cs = DN_CHUNK
    n = length // cs
    q = q.reshape(b, h, n, cs, dk)
    k = k.reshape(b, h, n, cs, dk)
    v = v.reshape(b, h, n, cs, dv)
    g = g.reshape(b, h, n, cs)
    beta = beta.reshape(b, h, n, cs)
    gc = jnp.cumsum(g, axis=-1)
    tri = jnp.tril(jnp.ones((cs, cs), dtype=bool))
    strict = jnp.tril(jnp.ones((cs, cs), dtype=bool), -1)
    diff = gc[..., :, None] - gc[..., None, :]
    decay = jnp.where(tri, jnp.exp(jnp.where(tri, diff, 0.0)), 0.0)
    kb = k * beta[..., None]
    vb = v * beta[..., None]
    a_kk = jnp.where(strict, jnp.einsum('bhncd,bhnsd->bhncs', kb, k) * decay, 0.0)
    eye = jnp.eye(cs, dtype=q.dtype)
    t_inv = lax.linalg.triangular_solve(
        eye + a_kk, jnp.broadcast_to(eye, a_kk.shape),
        left_side=True, lower=True, unit_diagonal=True)
    u = jnp.einsum('bhncs,bhnse->bhnce', t_inv, vb)
    w = jnp.einsum('bhncs,bhnsd->bhncd', t_inv, kb * jnp.exp(gc)[..., None])
    qg = q * jnp.exp(gc)[..., None]
    a_qk = jnp.einsum('bhncd,bhnsd->bhncs', q, k) * decay
    g_last = gc[..., -1]
    k_tail = k * jnp.exp(g_last[..., None] - gc)[..., None]

    def step(s, xs):
        u_n, w_n, qg_n, aqk_n, kt_n, gl_n = xs
        v_new = u_n - jnp.einsum('bhcd,bhde->bhce', w_n, s)
        o_n = jnp.einsum('bhcd,bhde->bhce', qg_n, s) + jnp.einsum('bhcs,bhse->bhce', aqk_n, v_new)
        s = s * jnp.exp(gl_n)[..., None, None] + jnp.einsum('bhcd,bhce->bhde', kt_n, v_new)
        return s, o_n

    xs = tuple(jnp.moveaxis(t, 2, 0) for t in (u, w, qg, a_qk, k_tail, g_last))
    s_fin, o = lax.scan(step, s0, xs)
    o = jnp.moveaxis(o, 0, 2).reshape(b, h, length, dv)
    return o, s_fin


def dn_prepare(qkv, ab, conv_w, a_log, dt_bias):
    b, length, _ = qkv.shape
    qkv = short_conv(qkv, conv_w).astype(jnp.float32)
    q = l2_normalize(qkv[..., :DN_QK_W].reshape(b, length, DN_HEADS, DN_DK)) * DN_DK ** -0.5
    k = l2_normalize(qkv[..., DN_QK_W:2 * DN_QK_W].reshape(b, length, DN_HEADS, DN_DK))
    v = qkv[..., 2 * DN_QK_W:].reshape(b, length, DN_HEADS, DN_DV)
    ab = ab.astype(jnp.float32).reshape(b, length, 2, 2, DN_HEADS)
    g = -jnp.exp(a_log.astype(jnp.float32)) * jax.nn.softplus(ab[:, :, 0] + dt_bias.astype(jnp.float32))
    beta = jax.nn.sigmoid(ab[:, :, 1])
    bhl = lambda t: jnp.transpose(t, (0, 2, 1, 3))
    return bhl(q), bhl(k), bhl(v), jnp.transpose(g, (2, 0, 3, 1)), jnp.transpose(beta, (2, 0, 3, 1))


def dn_output(o, gate, norm_w):
    b, h, length, dv = o.shape
    o = jnp.transpose(o, (0, 2, 1, 3))
    y = rms_norm(o, norm_w) * jax.nn.silu(gate.astype(jnp.float32)).reshape(b, length, h, dv)
    return y.reshape(b, length, h * dv).astype(gate.dtype)


def flip_seq(t):
    return jnp.flip(t, axis=2)


def deltanet_mixer(qkv_x, gate_x, ab_x, qkv_c, gate_c, ab_c, conv_w, a_log, dt_bias, norm_w, ctx_out):
    qx, kx, vx, g_x, b_x = dn_prepare(qkv_x, ab_x, conv_w, a_log, dt_bias)
    qc, kc, vc, g_c, b_c = dn_prepare(qkv_c, ab_c, conv_w, a_log, dt_bias)
    s0 = jnp.zeros((qx.shape[0], DN_HEADS, DN_DK, DN_DV), jnp.float32)
    oc_f, sc_f = chunk_gated_delta(qc, kc, vc, g_c[0], b_c[0], s0)
    oc_b, sc_b = chunk_gated_delta(flip_seq(qc), flip_seq(kc), flip_seq(vc),
                                   flip_seq(g_c[1]), flip_seq(b_c[1]), s0)
    ox_f, _ = chunk_gated_delta(qx, kx, vx, g_x[0], b_x[0], sc_f)
    ox_b, _ = chunk_gated_delta(flip_seq(qx), flip_seq(kx), flip_seq(vx),
                                flip_seq(g_x[1]), flip_seq(b_x[1]), sc_b)
    ox = dn_output(ox_f + flip_seq(ox_b), gate_x, norm_w)
    oc = dn_output(oc_f + flip_seq(oc_b), gate_c, norm_w) if ctx_out else None
    return ox, oc


def gqa_mixer(q_x, k_x, v_x, q_c, k_c, v_c, q_norm, k_norm, row, col, ctx_out):
    grp = GQA_HEADS // GQA_KV_HEADS
    scale = GQA_DIM ** -0.5

    def heads(t, n):
        b, length, _ = t.shape
        return jnp.transpose(t.reshape(b, length, n, GQA_DIM), (0, 2, 1, 3))

    kx = rope_2d(rms_norm(heads(k_x, GQA_KV_HEADS), k_norm), row, col)
    kc = rms_norm(heads(k_c, GQA_KV_HEADS), k_norm)
    vx = heads(v_x, GQA_KV_HEADS)
    vc = heads(v_c, GQA_KV_HEADS)
    k_all = jnp.concatenate([kc, kx], axis=2)
    v_all = jnp.concatenate([vc, vx], axis=2)
    qx = rope_2d(rms_norm(heads(q_x, GQA_HEADS), q_norm), row, col)
    b, _, length, _ = qx.shape
    ox = block_attention(qx.reshape(b, GQA_KV_HEADS, grp, length, GQA_DIM), k_all, v_all, scale)
    ox = merge_heads(ox.reshape(b, GQA_HEADS, length, GQA_DIM))
    oc = None
    if ctx_out:
        qc = rms_norm(heads(q_c, GQA_HEADS), q_norm)
        lc = qc.shape[2]
        oc = block_attention(qc.reshape(b, GQA_KV_HEADS, grp, lc, GQA_DIM), kc, vc, scale)
        oc = merge_heads(oc.reshape(b, GQA_HEADS, lc, GQA_DIM))
    return ox, oc


def mla_mixer(cq_x, ckv_x, kr_x, cq_c, ckv_c, kr_c, q_norm, kv_norm, w_uq, w_ukv, row, col, ctx_out):
    scale = (MLA_NOPE + MLA_ROPE) ** -0.5

    def queries(cq, rotate):
        b, length, _ = cq.shape
        q = (rms_norm(cq, q_norm) @ w_uq).reshape(b, length, MLA_HEADS, MLA_NOPE + MLA_ROPE)
        q = jnp.transpose(q, (0, 2, 1, 3))
        q_nope, q_rope = q[..., :MLA_NOPE], q[..., MLA_NOPE:]
        if rotate:
            q_rope = rope_2d(q_rope, row, col)
        return jnp.concatenate([q_nope, q_rope], axis=-1)[:, :, None]

    def keys_values(ckv, kr, rotate):
        b, length, _ = ckv.shape
        kv = (rms_norm(ckv, kv_norm) @ w_ukv).reshape(b, length, MLA_HEADS, MLA_NOPE + MLA_V)
        kv = jnp.transpose(kv, (0, 2, 1, 3))
        k_nope, v = kv[..., :MLA_NOPE], kv[..., MLA_NOPE:]
        kr = kr[:, None]
        if rotate:
            kr = rope_2d(kr, row, col)
        k = jnp.concatenate([k_nope, jnp.broadcast_to(kr, (b, MLA_HEADS, length, MLA_ROPE))], axis=-1)
        return k, v

    kx, vx = keys_values(ckv_x, kr_x, True)
    kc, vc = keys_values(ckv_c, kr_c, False)
    k_all = jnp.concatenate([kc, kx], axis=2)
    v_all = jnp.concatenate([vc, vx], axis=2)
    qx = queries(cq_x, True)
    b, _, _, length, _ = qx.shape
    ox = merge_heads(block_attention(qx, k_all, v_all, scale).reshape(b, MLA_HEADS, length, MLA_V))
    oc = None
    if ctx_out:
        qc = queries(cq_c, False)
        lc = qc.shape[3]
        oc = merge_heads(block_attention(qc, kc, vc, scale).reshape(b, MLA_HEADS, lc, MLA_V))
    return ox, oc


def moe_ffn(h, router_w, router_b, w_gate, b_gate, w_up, b_up, w_down, b_down):
    logits = (h @ router_w + router_b).astype(jnp.float32)
    top_val, top_idx = lax.top_k(logits, TOP_K)
    top_w = jax.nn.softmax(top_val, axis=-1)
    gates = jnp.sum(jax.nn.one_hot(top_idx, N_EXPERTS, dtype=jnp.float32) * top_w[..., None], axis=1)
    y = jnp.zeros(h.shape, jnp.float32)
    for e in range(N_EXPERTS):
        gl = jnp.minimum(h @ w_gate[e] + b_gate[e], SWIGLU_LIMIT)
        up = jnp.clip(h @ w_up[e] + b_up[e], -SWIGLU_LIMIT, SWIGLU_LIMIT)
        act = gl * jax.nn.sigmoid(SWIGLU_ALPHA * gl) * (up + 1.0)
        y = y + gates[:, e:e + 1] * (act @ w_down[e] + b_down[e]).astype(jnp.float32)
    return y.astype(h.dtype)


def modulation(cvec, w, b):
    return jnp.split(jax.nn.silu(cvec) @ w + b, 6, axis=-1)


def modulate(h, shift, scale):
    return layer_norm(h) * (1.0 + scale) + shift


def setup_inputs(seed: int = 0) -> dict:
    key = jax.random.key(seed)
    keys = jax.random.split(key, 30)
    d = D_MODEL

    def nrm(i, shape, scale):
        return scale * jax.random.normal(keys[i], shape, jnp.float32)

    def gain(i, shape):
        return 1.0 + nrm(i, shape, 0.02)

    dt = jnp.exp(jax.random.uniform(keys[9], (DEPTH, 2, DN_HEADS), jnp.float32,
                                    float(np.log(1e-3)), float(np.log(1e-1))))
    return {
        'x': nrm(0, (BATCH, SEQ, d), 1.0),
        'c': nrm(1, (BATCH, d), 1.0),
        'ctx': nrm(2, (BATCH, CTX_LEN, d), 1.0),
        'c_ctx': nrm(3, (d,), 1.0),
        'w_mod': nrm(4, (DEPTH, d, 6 * d), 0.5 * d ** -0.5),
        'b_mod': nrm(5, (DEPTH, 6 * d), 0.02),
        'w_in': nrm(6, (DEPTH, d, IN_WIDTH), d ** -0.5),
        'dn_conv': nrm(7, (DEPTH, DN_CONV, 2 * DN_QK_W + DN_V_W), DN_CONV ** -0.5),
        'dn_a_log': jnp.log(jax.random.uniform(keys[8], (DEPTH, 2, DN_HEADS), jnp.float32, 1.0, 16.0)),
        'dn_dt_bias': dt + jnp.log(-jnp.expm1(-dt)),
        'dn_norm': gain(10, (DEPTH, DN_DV)),
        'gqa_q_norm': gain(11, (DEPTH, GQA_DIM)),
        'gqa_k_norm': gain(12, (DEPTH, GQA_DIM)),
        'mla_q_norm': gain(13, (DEPTH, MLA_Q_RANK)),
        'mla_kv_norm': gain(14, (DEPTH, MLA_KV_RANK)),
        'mla_w_uq': nrm(15, (DEPTH, MLA_Q_RANK, MLA_HEADS * (MLA_NOPE + MLA_ROPE)), MLA_Q_RANK ** -0.5),
        'mla_w_ukv': nrm(16, (DEPTH, MLA_KV_RANK, MLA_HEADS * (MLA_NOPE + MLA_V)), MLA_KV_RANK ** -0.5),
        'w_out': nrm(17, (DEPTH, MIX_W, d), BETA_INIT * MIX_W ** -0.5),
        'ln1_g': gain(18, (DEPTH, d)),
        'ln1_b': nrm(19, (DEPTH, d), 0.02),
        'router_w': nrm(20, (DEPTH, d, N_EXPERTS), d ** -0.5),
        'router_b': nrm(21, (DEPTH, N_EXPERTS), 0.01),
        'exp_w_gate': nrm(22, (DEPTH, N_EXPERTS, d, D_EXPERT), d ** -0.5),
        'exp_b_gate': nrm(23, (DEPTH, N_EXPERTS, D_EXPERT), 0.02),
        'exp_w_up': nrm(24, (DEPTH, N_EXPERTS, d, D_EXPERT), d ** -0.5),
        'exp_b_up': nrm(25, (DEPTH, N_EXPERTS, D_EXPERT), 0.02),
        'exp_w_down': nrm(26, (DEPTH, N_EXPERTS, D_EXPERT, d), BETA_INIT * D_EXPERT ** -0.5),
        'exp_b_down': nrm(27, (DEPTH, N_EXPERTS, d), 0.02),
        'ln2_g': gain(28, (DEPTH, d)),
        'ln2_b': nrm(29, (DEPTH, d), 0.02),
    }


def reference(x, c, ctx, c_ctx, w_mod, b_mod, w_in, dn_conv, dn_a_log, dn_dt_bias, dn_norm,
              gqa_q_norm, gqa_k_norm, mla_q_norm, mla_kv_norm, mla_w_uq, mla_w_ukv, w_out,
              ln1_g, ln1_b, router_w, router_b, exp_w_gate, exp_b_gate, exp_w_up, exp_b_up,
              exp_w_down, exp_b_down, ln2_g, ln2_b):
    b, length, d = x.shape
    ROWS = length // GRID_W
    row = jnp.repeat(jnp.arange(ROWS, dtype=jnp.int32), GRID_W)
    col = jnp.tile(jnp.arange(GRID_W, dtype=jnp.int32), ROWS)
    for l in range(DEPTH):
        ctx_out = l < DEPTH - 1
        sh1, sc1, gt1, sh2, sc2, gt2 = [m[:, None] for m in modulation(c, w_mod[l], b_mod[l])]
        sh1c, sc1c, gt1c, sh2c, sc2c, gt2c = modulation(c_ctx, w_mod[l], b_mod[l])

        zx = split_columns(modulate(x, sh1, sc1) @ w_in[l])
        zc = split_columns(modulate(ctx, sh1c, sc1c) @ w_in[l])
        dn_x, dn_c = deltanet_mixer(zx[0], zx[1], zx[2], zc[0], zc[1], zc[2],
                                    dn_conv[l], dn_a_log[l], dn_dt_bias[l], dn_norm[l], ctx_out)
        gqa_x, gqa_c = gqa_mixer(zx[3], zx[4], zx[5], zc[3], zc[4], zc[5],
                                 gqa_q_norm[l], gqa_k_norm[l], row, col, ctx_out)
        mla_x, mla_c = mla_mixer(zx[6], zx[7], zx[8], zc[6], zc[7], zc[8],
                                 mla_q_norm[l], mla_kv_norm[l], mla_w_uq[l], mla_w_ukv[l],
                                 row, col, ctx_out)
        ox = jnp.concatenate([dn_x, gqa_x, mla_x], axis=-1) @ w_out[l]
        x = layer_norm(ALPHA * x + gt1 * ox, ln1_g[l], ln1_b[l])
        if ctx_out:
            oc = jnp.concatenate([dn_c, gqa_c, mla_c], axis=-1) @ w_out[l]
            ctx = layer_norm(ALPHA * ctx + gt1c * oc, ln1_g[l], ln1_b[l])

        hx = modulate(x, sh2, sc2).reshape(b * length, d)
        if ctx_out:
            lc = ctx.shape[1]
            hc = modulate(ctx, sh2c, sc2c).reshape(b * lc, d)
            y = moe_ffn(jnp.concatenate([hx, hc], axis=0), router_w[l], router_b[l],
                        exp_w_gate[l], exp_b_gate[l], exp_w_up[l], exp_b_up[l],
                        exp_w_down[l], exp_b_down[l])
            ctx = layer_norm(ALPHA * ctx + gt2c * y[b * length:].reshape(b, lc, d), ln2_g[l], ln2_b[l])
            y = y[:b * length]
        else:
            y = moe_ffn(hx, router_w[l], router_b[l], exp_w_gate[l], exp_b_gate[l],
                        exp_w_up[l], exp_b_up[l], exp_w_down[l], exp_b_down[l])
        x = layer_norm(ALPHA * x + gt2 * y.reshape(b, length, d), ln2_g[l], ln2_b[l])
    return x
```

```python
import functools

import jax
import jax.numpy as jnp
from jax import lax
from jax.experimental import pallas as pl
from jax.experimental.pallas import tpu as pltpu

F32 = jnp.float32
BF16 = jnp.bfloat16

GRID_W = 64
DN_HEADS = 4
DN_DK = 64
DN_DV = 64
DN_CHUNK = 64
GQA_HEADS = 8
GQA_KV_HEADS = 2
GQA_DIM = 64
MLA_HEADS = 4
MLA_Q_RANK = 192
MLA_KV_RANK = 128
MLA_NOPE = 64
MLA_ROPE = 32
MLA_V = 64
N_EXPERTS = 32
TOP_K = 4
SWIGLU_LIMIT = 7.0
SWIGLU_ALPHA = 1.702
ROPE_THETA = 10000.0
EPS = 1e-6

LANES = 128
HALF = 64
TOKEN_TILE = 256
MOE_TILE = 256
VMEM_LIMIT = 56 * 1024 * 1024

DN_W = 1280
DN_CONV_W = 1024
GQ_OFF = DN_W
GK_OFF = GQ_OFF + GQA_HEADS * LANES
GV_OFF = GK_OFF + GQA_KV_HEADS * LANES
CQ_OFF = GV_OFF + GQA_KV_HEADS * LANES
CKV_OFF = CQ_OFF + 2 * LANES
KR_OFF = CKV_OFF + LANES
IN_W = KR_OFF + LANES
AB_LANE = HALF


def _cparams(sem):
    return pltpu.CompilerParams(dimension_semantics=sem, vmem_limit_bytes=VMEM_LIMIT)


def _ln(x):
    mu = jnp.mean(x, axis=-1, keepdims=True)
    xc = x - mu
    var = jnp.mean(xc * xc, axis=-1, keepdims=True)
    return xc * lax.rsqrt(var + EPS)


def _silu(x):
    return x * jax.nn.sigmoid(x)


def _dot(a, b):
    return jnp.dot(a, b, preferred_element_type=F32)


def _dot_nt(a, b):
    return lax.dot_general(a, b, (((1,), (1,)), ((), ())), preferred_element_type=F32)


def _dot_tn(a, b):
    return lax.dot_general(a, b, (((0,), (0,)), ((), ())), preferred_element_type=F32)


def _split_bf16(x, parts):
    out = []
    r = x
    for i in range(parts):
        p = r.astype(BF16)
        out.append(p)
        if i + 1 < parts:
            r = r - p.astype(F32)
    return out


def _mod_kernel(c_ref, w_ref, b_ref, o_ref):
    s = _silu(c_ref[...]).astype(BF16)
    o_ref[0] = _dot(s, w_ref[0].astype(BF16)) + b_ref[0]


def _modulation(cvec, w_mod, b_mod):
    depth, d, n = w_mod.shape
    rows = cvec.shape[0]
    tn = 1536
    return pl.pallas_call(
        _mod_kernel,
        grid=(depth, n // tn),
        in_specs=[
            pl.BlockSpec((rows, d), lambda l, j: (0, 0)),
            pl.BlockSpec((1, d, tn), lambda l, j: (l, 0, j)),
            pl.BlockSpec((1, 1, tn), lambda l, j: (l, 0, j)),
        ],
        out_specs=pl.BlockSpec((1, rows, tn), lambda l, j: (l, 0, j)),
        out_shape=jax.ShapeDtypeStruct((depth, rows, n), F32),
        compiler_params=_cparams(("parallel", "parallel")),
        name="modulation",
    )(cvec, w_mod, b_mod.reshape(depth, 1, n))


def _rms_slab(xs, n, w):
    ms = jnp.sum(xs * xs, axis=-1, keepdims=True) * (1.0 / n)
    return xs * lax.rsqrt(ms + EPS) * w


def _rope(y, c, s_next, s_prev, h):
    return y * c + pltpu.roll(y, LANES - h, 1) * s_next + pltpu.roll(y, h, 1) * s_prev


def _inproj_kernel(x_ref, mod_ref, w_ref, tab_ref, gqn_ref, gkn_ref, mqn_ref, mkvn_ref,
                   wuq_ref, wukv_ref,
                   dn_ref, ab_ref, gq_ref, gk_ref, gv_ref, mq_ref, mk_ref, mv_ref):
    x = x_ref[0]
    shift = mod_ref[0, 0:1, :]
    scale = mod_ref[0, 1:2, :]
    h = (_ln(x) * (1.0 + scale) + shift).astype(BF16)
    z = _dot(h, w_ref[...])

    dn_ref[0] = z[:, 0:DN_W]
    ab_ref[0] = z[:, CQ_OFF + LANES:CQ_OFF + 2 * LANES]

    g_c, g_sn, g_sp = tab_ref[0], tab_ref[1], tab_ref[2]
    m_c, m_sn, m_sp = tab_ref[3], tab_ref[4], tab_ref[5]

    gqn = gqn_ref[...]
    for hh in range(GQA_HEADS):
        xs = z[:, GQ_OFF + hh * LANES:GQ_OFF + (hh + 1) * LANES]
        y = _rope(_rms_slab(xs, GQA_DIM, gqn), g_c, g_sn, g_sp, GQA_DIM // 4)
        gq_ref[0, :, hh * LANES:(hh + 1) * LANES] = (y * (GQA_DIM ** -0.5)).astype(BF16)
    gkn = gkn_ref[...]
    for g in range(GQA_KV_HEADS):
        xs = z[:, GK_OFF + g * LANES:GK_OFF + (g + 1) * LANES]
        y = _rope(_rms_slab(xs, GQA_DIM, gkn), g_c, g_sn, g_sp, GQA_DIM // 4)
        gk_ref[0, :, g * LANES:(g + 1) * LANES] = y.astype(BF16)
    gv_ref[0] = z[:, GV_OFF:GV_OFF + GQA_KV_HEADS * LANES].astype(BF16)

    cq = z[:, CQ_OFF:CQ_OFF + 2 * LANES]
    lane = lax.broadcasted_iota(jnp.int32, cq.shape, 1)
    cq = jnp.where(lane < MLA_Q_RANK, cq, 0.0)
    cqn = _rms_slab(cq, MLA_Q_RANK, mqn_ref[...]).astype(BF16)
    q = _dot(cqn, wuq_ref[...])
    for hh in range(MLA_HEADS):
        y = _rope(q[:, hh * LANES:(hh + 1) * LANES], m_c, m_sn, m_sp, MLA_ROPE // 4)
        mq_ref[0, :, hh * LANES:(hh + 1) * LANES] = y.astype(BF16)

    ckv = z[:, CKV_OFF:CKV_OFF + LANES]
    ckvn = _rms_slab(ckv, MLA_KV_RANK, mkvn_ref[...]).astype(BF16)
    kv = _dot(ckvn, wukv_ref[...])
    kr = _rope(z[:, KR_OFF:KR_OFF + LANES], m_c, m_sn, m_sp, MLA_ROPE // 4)
    for hh in range(MLA_HEADS):
        mk_ref[0, :, hh * LANES:(hh + 1) * LANES] = (kv[:, hh * LANES:(hh + 1) * LANES] + kr).astype(BF16)
    mv_ref[0] = kv[:, MLA_HEADS * LANES:].astype(BF16)


def _inproj(xall, mod, w_in_p, tabs, gqn, gkn, mqn, mkvn, wuq_p, wukv_p, ctx_tiles):
    b, l, d = xall.shape
    tm = TOKEN_TILE
    nt = l // tm
    n_ctx_row = mod.shape[0] - 1

    def mod_idx(bi, j):
        return (jnp.where(j < ctx_tiles, n_ctx_row, bi), 0, 0)

    row = lambda bi, j: (bi, j, 0)
    const2 = lambda bi, j: (0, 0)
    widths = (DN_W, LANES, GQA_HEADS * LANES, GQA_KV_HEADS * LANES, GQA_KV_HEADS * LANES,
              MLA_HEADS * LANES, MLA_HEADS * LANES, MLA_HEADS * LANES)
    dtypes = (F32, F32, BF16, BF16, BF16, BF16, BF16, BF16)
    return pl.pallas_call(
        _inproj_kernel,
        grid=(b, nt),
        in_specs=[
            pl.BlockSpec((1, tm, d), row),
            pl.BlockSpec((1, 6, d), mod_idx),
            pl.BlockSpec((d, IN_W), const2),
            pl.BlockSpec((6, tm, LANES), lambda bi, j: (0, j, 0)),
            pl.BlockSpec((1, LANES), const2),
            pl.BlockSpec((1, LANES), const2),
            pl.BlockSpec((1, 2 * LANES), const2),
            pl.BlockSpec((1, LANES), const2),
            pl.BlockSpec((2 * LANES, MLA_HEADS * LANES), const2),
            pl.BlockSpec((LANES, 2 * MLA_HEADS * LANES), const2),
        ],
        out_specs=[pl.BlockSpec((1, tm, w), row) for w in widths],
        out_shape=[jax.ShapeDtypeStruct((b, l, w), dt) for w, dt in zip(widths, dtypes)],
        compiler_params=_cparams(("parallel", "parallel")),
        name="inproj",
    )(xall, mod, w_in_p, tabs, gqn, gkn, mqn, mkvn, wuq_p, wukv_p)


def _attn_kernel(q_ref, k_ref, v_ref, o_ref, *, n_heads, grp, scale, ctx_len, kv_len, with_ctx):
    def run(lk):
        outs = []
        for h in range(n_heads):
            g = h // grp
            q = q_ref[0, :, h * LANES:(h + 1) * LANES]
            k = k_ref[0, 0:lk, g * LANES:(g + 1) * LANES]
            v = v_ref[0, 0:lk, g * LANES:(g + 1) * LANES]
            s = _dot_nt(q, k)
            if scale != 1.0:
                s = s * scale
            m = jnp.max(s, axis=-1, keepdims=True)
            p = jnp.exp(s - m)
            den = jnp.sum(p, axis=-1, keepdims=True)
            outs.append(_dot(p.astype(BF16), v) / den)
        lane = lax.broadcasted_iota(jnp.int32, outs[0].shape, 1)
        for j in range(n_heads // 2):
            o_ref[0, :, j * LANES:(j + 1) * LANES] = jnp.where(
                lane < HALF, outs[2 * j], outs[2 * j + 1]).astype(o_ref.dtype)

    if with_ctx:
        j = pl.program_id(1)
        n_ctx_tiles = ctx_len // TOKEN_TILE

        @pl.when(j < n_ctx_tiles)
        def _():
            run(ctx_len)

        @pl.when(j >= n_ctx_tiles)
        def _():
            run(kv_len)
    else:
        run(kv_len)


def _attention(q, k, v, *, n_heads, grp, scale, ctx_len, with_ctx, name):
    b, l, _ = q.shape
    tq = TOKEN_TILE
    ctx_tiles = ctx_len // tq
    off = 0 if with_ctx else ctx_tiles
    kw = k.shape[-1]
    ow = n_heads * HALF
    kern = functools.partial(_attn_kernel, n_heads=n_heads, grp=grp, scale=scale,
                             ctx_len=ctx_len, kv_len=l, with_ctx=with_ctx)
    return pl.pallas_call(
        kern,
        grid=(b, l // tq - off),
        in_specs=[
            pl.BlockSpec((1, tq, n_heads * LANES), lambda bi, j: (bi, j + off, 0)),
            pl.BlockSpec((1, l, kw), lambda bi, j: (bi, 0, 0)),
            pl.BlockSpec((1, l, kw), lambda bi, j: (bi, 0, 0)),
        ],
        out_specs=pl.BlockSpec((1, tq, ow), lambda bi, j: (bi, j, 0)),
        out_shape=jax.ShapeDtypeStruct((b, l - off * tq, ow), BF16),
        compiler_params=_cparams(("parallel", "parallel")),
        name=name,
    )(q, k, v)


def _softplus(x):
    return jnp.maximum(x, 0.0) + jnp.log(1.0 + jnp.exp(-jnp.abs(x)))


def _deltanet_kernel(dn_ref, ab_ref, abt_ref, cw_ref, garow_ref, gacol_ref, nw_ref, o_ref,
                     s_scr, of_scr, ob_scr, *, seq_len, ctx_len):
    cs = DN_CHUNK
    n_chunks = seq_len // cs
    nc_ctx = ctx_len // cs
    n_units = 2 * DN_HEADS

    ri = lax.broadcasted_iota(jnp.int32, (cs, cs), 0)
    ci = lax.broadcasted_iota(jnp.int32, (cs, cs), 1)
    tril = (ci <= ri)
    triu = (ci >= ri)
    tril_b = tril.astype(BF16)
    triu_b = triu.astype(BF16)
    li = lax.broadcasted_iota(jnp.int32, (LANES, LANES), 0)
    lj = lax.broadcasted_iota(jnp.int32, (LANES, LANES), 1)
    blockdiag = ((li >= HALF) == (lj >= HALF)).astype(BF16)
    lane = lax.broadcasted_iota(jnp.int32, (cs, LANES), 1)
    upper = lane >= HALF
    rowid = lax.broadcasted_iota(jnp.int32, (cs, DN_CONV_W), 0)

    cw0 = cw_ref[0:1, :]
    cw1 = cw_ref[1:2, :]
    cw2 = cw_ref[2:3, :]
    neg_a_row = garow_ref[0:1, :]
    dtb_row = garow_ref[1:2, :]
    neg_a_col = gacol_ref[:, 0:1]
    dtb_col = gacol_ref[:, 1:2]

    s_scr[...] = jnp.zeros(s_scr.shape, F32)

    def chunk_prep(cidx):
        r0 = pl.multiple_of(cidx * cs, cs)
        cur = dn_ref[0, pl.ds(r0, cs), 0:DN_CONV_W]
        prow = dn_ref[0, pl.ds(jnp.maximum(r0 - 1, 0), 1), 0:DN_CONV_W]
        nrow = dn_ref[0, pl.ds(jnp.minimum(r0 + cs, seq_len - 1), 1), 0:DN_CONV_W]
        has_prev = jnp.logical_and(cidx != 0, cidx != nc_ctx)
        has_next = jnp.logical_and(cidx != nc_ctx - 1, cidx != n_chunks - 1)
        prow = jnp.where(has_prev, prow, 0.0)
        nrow = jnp.where(has_next, nrow, 0.0)
        xm = jnp.where(rowid == 0, prow, pltpu.roll(cur, 1, 0))
        xp = jnp.where(rowid == cs - 1, nrow, pltpu.roll(cur, cs - 1, 0))
        y = _silu(cw0 * xm + cw1 * cur + cw2 * xp)
        a_slabs, b_slabs = [], []
        for s in range(2 * DN_HEADS):
            ys = y[:, s * LANES:(s + 1) * LANES]
            sq = ys * ys
            ss = sum(_dot(p, blockdiag) for p in _split_bf16(sq, 2))
            inv = lax.rsqrt(ss + EPS)
            if s < DN_HEADS:
                a_slabs.append(ys * jnp.where(upper, inv, 1.0))
            else:
                b_slabs.append(ys * inv * (DN_DK ** -0.5))
        ab = ab_ref[0, pl.ds(r0, cs), :]
        gslab = neg_a_row * _softplus(ab + dtb_row)
        beta = jax.nn.sigmoid(ab)
        gparts = _split_bf16(gslab, 3)
        gc_f = sum(_dot(tril_b, p) for p in gparts)
        gc_r = sum(_dot(triu_b, p) for p in gparts)
        abt = abt_ref[0, cidx]
        gt = neg_a_col * _softplus(abt + dtb_col)
        gtparts = _split_bf16(gt, 3)
        gct_f = sum(_dot(p, triu_b) for p in gtparts)
        gct_r = sum(_dot(p, tril_b) for p in gtparts)
        return a_slabs, b_slabs, beta, (gc_f, gc_r), (gct_f, gct_r)

    def unit(d, h, prep):
        a_slabs, b_slabs, beta, gcs, gcts = prep
        c = d * DN_HEADS + h
        a_h, b_h = a_slabs[h], b_slabs[h]
        gcol = gcs[d][:, AB_LANE + c:AB_LANE + c + 1]
        grow = gcts[d][c:c + 1, :]
        bcol = beta[:, AB_LANE + n_units + c:AB_LANE + n_units + c + 1]
        glast = grow[:, cs - 1:cs] if d == 0 else grow[:, 0:1]
        mask = tril if d == 0 else triu
        strict = (ci < ri) if d == 0 else (ci > ri)
        decay = jnp.where(mask, jnp.exp(jnp.where(mask, gcol - grow, 0.0)), 0.0)
        egc = jnp.exp(gcol)
        a_k = jnp.where(upper, a_h, 0.0)
        b_q = jnp.where(upper, b_h, 0.0)
        a_kb = a_k.astype(BF16)
        gram = _dot_nt(jnp.concatenate([b_q, a_k], axis=0).astype(BF16), a_kb)
        a_qk = gram[0:cs] * decay
        nm = -jnp.where(strict, gram[cs:2 * cs] * bcol * decay, 0.0)
        x = a_h * jnp.where(upper, bcol * egc, bcol)
        for i in range(6):
            nb = nm.astype(BF16)
            x = x + _dot(nb, x.astype(BF16))
            if i < 5:
                nm = _dot(nb, nb)
        s_old = s_scr[c]
        qg = b_q * egc
        both = _dot(jnp.concatenate([jnp.where(upper, x, 0.0), qg], axis=0).astype(BF16),
                    s_old.astype(BF16))
        v_new = x[:, 0:DN_DV] - both[0:cs]
        o = both[cs:2 * cs] + _dot(a_qk.astype(BF16), v_new.astype(BF16))
        k_tail = (a_k * jnp.exp(glast - gcol)).astype(BF16)
        s_scr[c] = s_old * jnp.exp(glast) + _dot_tn(k_tail, v_new.astype(BF16))
        return o

    def body(n, carry):
        cf = n
        cb = jnp.where(n < nc_ctx, nc_ctx - 1 - n, n_chunks + nc_ctx - 1 - n)
        prep_f = chunk_prep(cf)
        prep_b = chunk_prep(cb)
        rf = pl.multiple_of(cf * cs, cs)
        rb = pl.multiple_of(cb * cs, cs)
        for h in range(DN_HEADS):
            of_scr[h, pl.ds(rf, cs), :] = unit(0, h, prep_f)
            ob_scr[h, pl.ds(rb, cs), :] = unit(1, h, prep_b)
        return carry

    lax.fori_loop(0, n_chunks, body, 0)

    nw = nw_ref[...]
    rt = TOKEN_TILE

    def out_body(t, carry):
        r0 = pl.multiple_of(t * rt, rt)
        gate = dn_ref[0, pl.ds(r0, rt), DN_CONV_W:DN_W]
        sg = _silu(gate)
        ys = []
        for h in range(DN_HEADS):
            o = of_scr[h, pl.ds(r0, rt), :] + ob_scr[h, pl.ds(r0, rt), :]
            ms = jnp.mean(o * o, axis=-1, keepdims=True)
            ys.append(o * lax.rsqrt(ms + EPS) * nw * sg[:, h * DN_DV:(h + 1) * DN_DV])
        o_ref[0, pl.ds(r0, rt), :] = jnp.concatenate(ys, axis=-1).astype(o_ref.dtype)
        return carry

    lax.fori_loop(0, seq_len // rt, out_body, 0)


def _deltanet(dn, ab, abt, cw, garow, gacol, nw, ctx_len):
    b, l, _ = dn.shape
    n_chunks = l // DN_CHUNK
    kern = functools.partial(_deltanet_kernel, seq_len=l, ctx_len=ctx_len)
    return pl.pallas_call(
        kern,
        grid=(b,),
        in_specs=[
            pl.BlockSpec((1, l, DN_W), lambda bi: (bi, 0, 0)),
            pl.BlockSpec((1, l, LANES), lambda bi: (bi, 0, 0)),
            pl.BlockSpec((1, n_chunks, 16, DN_CHUNK), lambda bi: (bi, 0, 0, 0)),
            pl.BlockSpec((3, DN_CONV_W), lambda bi: (0, 0)),
            pl.BlockSpec((2, LANES), lambda bi: (0, 0)),
            pl.BlockSpec((16, 2), lambda bi: (0, 0)),
            pl.BlockSpec((1, DN_DV), lambda bi: (0, 0)),
        ],
        out_specs=pl.BlockSpec((1, l, DN_HEADS * DN_DV), lambda bi: (bi, 0, 0)),
        out_shape=jax.ShapeDtypeStruct((b, l, DN_HEADS * DN_DV), BF16),
        scratch_shapes=[
            pltpu.VMEM((2 * DN_HEADS, 2 * DN_DK, DN_DV), F32),
            pltpu.VMEM((DN_HEADS, l, DN_DV), F32),
            pltpu.VMEM((DN_HEADS, l, DN_DV), F32),
        ],
        compiler_params=_cparams(("parallel",)),
        name="deltanet",
    )(dn, ab, abt, cw, garow, gacol, nw)


def _outproj_kernel(x_ref, dno_ref, gqo_ref, mlo_ref, mod_ref, w_ref, g_ref, b_ref, rw_ref, rb_ref,
                    x1_ref, hx_ref, idx_ref, wt_ref, *, alpha):
    mix = jnp.concatenate([dno_ref[0], gqo_ref[0], mlo_ref[0]], axis=-1)
    o = _dot(mix, w_ref[...])
    gate1 = mod_ref[0, 2:3, :]
    x1 = _ln(alpha * x_ref[0] + gate1 * o) * g_ref[...] + b_ref[...]
    x1_ref[...] = x1
    hx = _ln(x1) * (1.0 + mod_ref[0, 4:5, :]) + mod_ref[0, 3:4, :]
    hx_ref[...] = hx
    logits = _dot(hx.astype(BF16), rw_ref[...]) + rb_ref[...]
    lane = lax.broadcasted_iota(jnp.int32, logits.shape, 1)
    vals, idxs = [], []
    cur = logits
    for _ in range(TOP_K):
        m = jnp.max(cur, axis=-1, keepdims=True)
        idx = jnp.min(jnp.where(cur == m, lane, LANES), axis=-1, keepdims=True)
        vals.append(m)
        idxs.append(idx)
        cur = jnp.where(lane == idx, -jnp.inf, cur)
    es = [jnp.exp(v - vals[0]) for v in vals]
    den = es[0] + es[1] + es[2] + es[3]
    idx_out = jnp.zeros(logits.shape, jnp.int32)
    wt_out = jnp.zeros(logits.shape, F32)
    for kk in range(TOP_K):
        idx_out = jnp.where(lane == kk, idxs[kk], idx_out)
        wt_out = jnp.where(lane == kk, es[kk] / den, wt_out)
    idx_ref[...] = idx_out
    wt_ref[...] = wt_out


def _outproj(xall, dno, gqo, mlo, mod, w_out, ln_g, ln_b, rw, rb, ctx_tiles, with_ctx, alpha):
    b, l, d = xall.shape
    tm = TOKEN_TILE
    nt = l // tm
    off = 0 if with_ctx else ctx_tiles
    ntl = nt - off
    n_ctx_row = mod.shape[0] - 1

    def mod_idx(bi, j):
        return (jnp.where(j + off < ctx_tiles, n_ctx_row, bi), 0, 0)

    row = lambda bi, j: (bi, j + off, 0)
    qrow = lambda bi, j: (bi, j, 0)
    flat = lambda bi, j: (bi * ntl + j, 0)
    const2 = lambda bi, j: (0, 0)
    t = b * ntl * tm
    return pl.pallas_call(
        functools.partial(_outproj_kernel, alpha=alpha),
        grid=(b, ntl),
        in_specs=[
            pl.BlockSpec((1, tm, d), row),
            pl.BlockSpec((1, tm, dno.shape[-1]), row),
            pl.BlockSpec((1, tm, gqo.shape[-1]), qrow),
            pl.BlockSpec((1, tm, mlo.shape[-1]), qrow),
            pl.BlockSpec((1, 6, d), mod_idx),
            pl.BlockSpec((d, d), const2),
            pl.BlockSpec((1, d), const2),
            pl.BlockSpec((1, d), const2),
            pl.BlockSpec((d, LANES), const2),
            pl.BlockSpec((1, LANES), const2),
        ],
        out_specs=[
            pl.BlockSpec((tm, d), flat),
            pl.BlockSpec((tm, d), flat),
            pl.BlockSpec((tm, LANES), flat),
            pl.BlockSpec((tm, LANES), flat),
        ],
        out_shape=[
            jax.ShapeDtypeStruct((t, d), F32),
            jax.ShapeDtypeStruct((t, d), F32),
            jax.ShapeDtypeStruct((t, LANES), jnp.int32),
            jax.ShapeDtypeStruct((t, LANES), F32),
        ],
        compiler_params=_cparams(("parallel", "parallel")),
        name="outproj_router",
    )(xall, dno, gqo, mlo, mod, w_out, ln_g, ln_b, rw, rb)


def _moe_kernel(te_ref, nv_ref,
                tok_ref, tokn_ref, dst_ref,
                hx_hbm, gate_ref, wg_ref, bg_ref, wu_ref, bu_ref, wd_ref, bd_ref,
                y_hbm,
                xbuf, obuf, wg_b, wu_b, wd_b, gsem, ssem):
    i = pl.program_id(0)
    n_valid = nv_ref[0]
    tm = MOE_TILE
    slot = lax.rem(i, 2)

    def start_gather(idx_ref, s):
        def body(r, c):
            tok = idx_ref[0, 0, r]
            pltpu.make_async_copy(hx_hbm.at[pl.ds(tok, 1), :], xbuf.at[s, pl.ds(r, 1), :], gsem.at[s]).start()
            return c
        lax.fori_loop(0, tm, body, 0, unroll=8)

    def wait_gather(s):
        pltpu.make_async_copy(xbuf.at[s], xbuf.at[s], gsem.at[s]).wait()

    def start_scatter(s):
        def body(r, c):
            dst = dst_ref[0, 0, r]
            pltpu.make_async_copy(obuf.at[s, pl.ds(r, 1), :], y_hbm.at[pl.ds(dst, 1), :], ssem.at[s]).start()
            return c
        lax.fori_loop(0, tm, body, 0, unroll=8)

    def wait_scatter(s):
        pltpu.make_async_copy(obuf.at[s], obuf.at[s], ssem.at[s]).wait()

    @pl.when(i == 0)
    def _():
        start_gather(tok_ref, 0)
        n_real = y_hbm.shape[0] - 2 * tm
        obuf[...] = jnp.zeros(obuf.shape, F32)
        for s in range(2):
            spare = pltpu.make_async_copy(obuf.at[s], y_hbm.at[pl.ds(n_real + s * tm, tm), :], ssem.at[s])
            spare.start()
            spare.wait()

    @pl.when(i + 1 < n_valid)
    def _():
        start_gather(tokn_ref, 1 - slot)

    @pl.when(i < n_valid)
    def _():
        prev_e = te_ref[jnp.maximum(i - 1, 0)]
        new_expert = jnp.logical_or(i == 0, te_ref[i] != prev_e)

        @pl.when(new_expert)
        def _():
            wg_b[...] = wg_ref[0].astype(BF16)
            wu_b[...] = wu_ref[0].astype(BF16)
            wd_b[...] = wd_ref[0].astype(BF16)

        wait_gather(slot)

        @pl.when(i >= 2)
        def _():
            wait_scatter(slot)

        xb = xbuf[slot].astype(BF16)
        gl = jnp.minimum(_dot(xb, wg_b[...]) + bg_ref[0], SWIGLU_LIMIT)
        up = jnp.clip(_dot(xb, wu_b[...]) + bu_ref[0], -SWIGLU_LIMIT, SWIGLU_LIMIT)
        act = gl * jax.nn.sigmoid(SWIGLU_ALPHA * gl) * (up + 1.0)
        out = _dot(act.astype(BF16), wd_b[...]) + bd_ref[0]
        obuf[slot] = out * gate_ref[...]
        start_scatter(slot)

        @pl.when(i == n_valid - 1)
        def _():
            @pl.when(i >= 1)
            def _():
                wait_scatter(1 - slot)
            wait_scatter(slot)


def _moe(hx, tile_e, n_valid, row_tok, row_dst, row_gate, wg, bg, wu, bu, wd, bd, n_tiles):
    t, d = hx.shape
    e, _, de = wg.shape
    tm = MOE_TILE
    wspec = lambda shp: pl.BlockSpec(shp, lambda i, te, nv: (te[i], 0, 0))
    ispec = lambda nxt: pl.BlockSpec((1, 1, tm), lambda i, te, nv: (jnp.minimum(i + nxt, n_tiles - 1), 0, 0),
                                     memory_space=pltpu.SMEM)
    grid_spec = pltpu.PrefetchScalarGridSpec(
        num_scalar_prefetch=2,
        grid=(n_tiles,),
        in_specs=[
            ispec(0), ispec(1), ispec(0),
            pl.BlockSpec(memory_space=pl.ANY),
            pl.BlockSpec((tm, 1), lambda i, te, nv: (i, 0)),
            wspec((1, d, de)), wspec((1, 1, de)),
            wspec((1, d, de)), wspec((1, 1, de)),
            wspec((1, de, d)), wspec((1, 1, d)),
        ],
        out_specs=pl.BlockSpec(memory_space=pl.ANY),
        scratch_shapes=[
            pltpu.VMEM((2, tm, d), F32),
            pltpu.VMEM((2, tm, d), F32),
            pltpu.VMEM((d, de), BF16),
            pltpu.VMEM((d, de), BF16),
            pltpu.VMEM((de, d), BF16),
            pltpu.SemaphoreType.DMA((2,)),
            pltpu.SemaphoreType.DMA((2,)),
        ],
    )
    tok3 = row_tok.reshape(n_tiles, 1, tm)
    return pl.pallas_call(
        _moe_kernel,
        grid_spec=grid_spec,
        out_shape=jax.ShapeDtypeStruct((TOP_K * t + 2 * tm, d), F32),
        compiler_params=_cparams(("arbitrary",)),
        name="moe_experts",
    )(tile_e, n_valid, tok3, tok3, row_dst.reshape(n_tiles, 1, tm), hx, row_gate,
      wg, bg.reshape(e, 1, de), wu, bu.reshape(e, 1, de), wd, bd.reshape(e, 1, d))


def _route(top_idx, top_w, t, n_tiles):
    tm = MOE_TILE
    nf = TOP_K * t
    flat_e = top_idx.reshape(nf)
    fid = jnp.arange(nf, dtype=jnp.int32)
    order = jnp.sort(flat_e * nf + fid) % nf
    counts = jnp.sum((flat_e[:, None] == jnp.arange(N_EXPERTS, dtype=jnp.int32)[None, :]).astype(jnp.int32), axis=0)
    tiles_per_e = (counts + tm - 1) // tm
    tile_end = jnp.cumsum(tiles_per_e)
    tile_off = tile_end - tiles_per_e
    real_off = jnp.cumsum(counts) - counts
    n_valid = tile_end[-1]
    tile_id = jnp.arange(n_tiles, dtype=jnp.int32)
    tile_e = jnp.sum((tile_id[:, None] >= tile_end[None, :]).astype(jnp.int32), axis=1)
    last_e = jnp.sum((n_valid - 1 >= tile_end).astype(jnp.int32))
    tile_e = jnp.where(tile_id < n_valid, tile_e, last_e).astype(jnp.int32)
    slot = jnp.arange(n_tiles * tm, dtype=jnp.int32)
    se = tile_e[slot // tm]
    rank = slot - tile_off[se] * tm
    valid = jnp.logical_and(rank < counts[se], slot // tm < n_valid)
    f = order[jnp.clip(real_off[se] + rank, 0, nf - 1)]
    tok = jnp.where(valid, f // TOP_K, 0)
    spare = nf + ((slot // tm) % 2) * tm + slot % tm
    dst = jnp.where(valid, (f % TOP_K) * t + f // TOP_K, spare)
    gate = jnp.where(valid, top_w.reshape(nf)[f], 0.0)
    return (tile_e, n_valid.reshape(1).astype(jnp.int32), tok.astype(jnp.int32),
            dst.astype(jnp.int32), gate.reshape(n_tiles * tm, 1).astype(F32))


def _final_kernel(x1_ref, y0_ref, y1_ref, y2_ref, y3_ref, mod_ref, g_ref, b_ref, o_ref, *, alpha):
    y = (y0_ref[...] + y1_ref[...]) + (y2_ref[...] + y3_ref[...])
    gate2 = mod_ref[0, 5:6, :]
    o_ref[0] = _ln(alpha * x1_ref[...] + gate2 * y) * g_ref[...] + b_ref[...]


def _final(x1, y, mod, ln_g, ln_b, b, ctx_tiles, with_ctx, alpha):
    t, d = x1.shape
    tm = TOKEN_TILE
    ntl = t // (b * tm)
    tt = t // tm
    off = 0 if with_ctx else ctx_tiles
    n_ctx_row = mod.shape[0] - 1

    def mod_idx(bi, j):
        return (jnp.where(j + off < ctx_tiles, n_ctx_row, bi), 0, 0)

    yspec = lambda kk: pl.BlockSpec((tm, d), lambda bi, j: (kk * tt + bi * ntl + j, 0))
    return pl.pallas_call(
        functools.partial(_final_kernel, alpha=alpha),
        grid=(b, ntl),
        in_specs=[
            pl.BlockSpec((tm, d), lambda bi, j: (bi * ntl + j, 0)),
            yspec(0), yspec(1), yspec(2), yspec(3),
            pl.BlockSpec((1, 6, d), mod_idx),
            pl.BlockSpec((1, d), lambda bi, j: (0, 0)),
            pl.BlockSpec((1, d), lambda bi, j: (0, 0)),
        ],
        out_specs=pl.BlockSpec((1, tm, d), lambda bi, j: (bi, j, 0)),
        out_shape=jax.ShapeDtypeStruct((b, ntl * tm, d), F32),
        compiler_params=_cparams(("parallel", "parallel")),
        name="moe_combine_ln",
    )(x1, y, y, y, y, mod, ln_g, ln_b)


def _pack_w_in(w):
    d = w.shape[0]
    z = lambda n: jnp.zeros((d, n), w.dtype)
    qk_w = DN_HEADS * DN_DK
    o_q, o_k, o_v = 0, qk_w, 2 * qk_w
    o_gate = 2 * qk_w + DN_HEADS * DN_DV
    o_ab = o_gate + DN_HEADS * DN_DV
    o_gq = o_ab + 4 * DN_HEADS
    o_gk = o_gq + GQA_HEADS * GQA_DIM
    o_gv = o_gk + GQA_KV_HEADS * GQA_DIM
    o_cq = o_gv + GQA_KV_HEADS * GQA_DIM
    o_ckv = o_cq + MLA_Q_RANK
    o_kr = o_ckv + MLA_KV_RANK
    col = lambda o, i, n: w[:, o + i * n:o + (i + 1) * n]
    parts = []
    for h in range(DN_HEADS):
        parts += [col(o_v, h, DN_DV), col(o_k, h, DN_DK)]
    for h in range(DN_HEADS):
        parts += [col(o_q, h, DN_DK), col(o_q, h, DN_DK)]
    parts.append(w[:, o_gate:o_gate + DN_HEADS * DN_DV])
    for h in range(GQA_HEADS):
        parts += [col(o_gq, h, GQA_DIM), z(HALF)]
    for g in range(GQA_KV_HEADS):
        parts += [col(o_gk, g, GQA_DIM), z(HALF)]
    for g in range(GQA_KV_HEADS):
        parts += [col(o_gv, g, GQA_DIM), col(o_gv, g, GQA_DIM)]
    parts += [w[:, o_cq:o_cq + MLA_Q_RANK], w[:, o_ab:o_ab + 4 * DN_HEADS],
              z(2 * LANES - MLA_Q_RANK - 4 * DN_HEADS)]
    parts.append(w[:, o_ckv:o_ckv + MLA_KV_RANK])
    parts += [z(HALF), w[:, o_kr:o_kr + MLA_ROPE], z(LANES - HALF - MLA_ROPE)]
    out = jnp.concatenate(parts, axis=1)
    assert out.shape[1] == IN_W, out.shape
    return out.astype(BF16)


def _pack_conv(cw):
    qk_w = DN_HEADS * DN_DK
    col = lambda o, i, n: cw[:, o + i * n:o + (i + 1) * n]
    parts = []
    for h in range(DN_HEADS):
        parts += [col(2 * qk_w, h, DN_DV), col(qk_w, h, DN_DK)]
    for h in range(DN_HEADS):
        parts += [col(0, h, DN_DK), col(0, h, DN_DK)]
    return jnp.concatenate(parts, axis=1)


def _pack_wuq(w):
    qd = MLA_NOPE + MLA_ROPE
    parts = []
    for h in range(MLA_HEADS):
        parts += [w[:, h * qd:(h + 1) * qd], jnp.zeros((w.shape[0], LANES - qd), w.dtype)]
    out = jnp.concatenate(parts, axis=1)
    out = jnp.concatenate([out, jnp.zeros((2 * LANES - MLA_Q_RANK, out.shape[1]), w.dtype)], axis=0)
    return out.astype(BF16)


def _pack_wukv(w):
    kd = MLA_NOPE + MLA_V
    kparts, vparts = [], []
    for h in range(MLA_HEADS):
        kparts += [w[:, h * kd:h * kd + MLA_NOPE], jnp.zeros((w.shape[0], LANES - MLA_NOPE), w.dtype)]
        v = w[:, h * kd + MLA_NOPE:(h + 1) * kd]
        vparts += [v, v]
    return jnp.concatenate(kparts + vparts, axis=1).astype(BF16)


def _rope_tables(ctx_len, seq):
    t = jnp.arange(seq, dtype=jnp.int32)
    row = (t // GRID_W).astype(F32)[:, None]
    col = (t % GRID_W).astype(F32)[:, None]
    lane = jnp.arange(LANES, dtype=jnp.int32)[None, :]

    def tables(first, width, half):
        dd = lane - first
        active = jnp.logical_and(dd >= 0, dd < width)
        dd = jnp.clip(dd, 0, width - 1)
        use_col = dd >= width // 2
        p = dd % (width // 2)
        fi = (p % half).astype(F32)
        inv = ROPE_THETA ** (-fi / half)
        ang = jnp.where(use_col, col, row) * inv
        cos = jnp.where(active, jnp.cos(ang), 1.0)
        sin = jnp.where(active, jnp.sin(ang), 0.0)
        s_next = jnp.where(p < half, -sin, 0.0)
        s_prev = jnp.where(p >= half, sin, 0.0)
        return [cos, s_next, s_prev]

    lat = jnp.stack(tables(0, GQA_DIM, GQA_DIM // 4) + tables(HALF, MLA_ROPE, MLA_ROPE // 4))
    ident = jnp.stack([jnp.ones((ctx_len, LANES), F32), jnp.zeros((ctx_len, LANES), F32),
                       jnp.zeros((ctx_len, LANES), F32)] * 2)
    return jnp.concatenate([ident, lat], axis=1)


def _pad_lanes(v, n):
    return jnp.concatenate([v, jnp.zeros((n - v.shape[0],), v.dtype)]).reshape(1, n)


def kernel(x, c, ctx, c_ctx, w_mod, b_mod, w_in, dn_conv, dn_a_log, dn_dt_bias, dn_norm, gqa_q_norm, gqa_k_norm, mla_q_norm, mla_kv_norm, mla_w_uq, mla_w_ukv, w_out, ln1_g, ln1_b, router_w, router_b, exp_w_gate, exp_b_gate, exp_w_up, exp_b_up, exp_w_down, exp_b_down, ln2_g, ln2_b):
    b, seq, d = x.shape
    ctx_len = ctx.shape[1]
    depth = w_mod.shape[0]
    l = ctx_len + seq
    tm = TOKEN_TILE
    assert ctx_len % tm == 0 and seq % tm == 0 and seq % GRID_W == 0
    ctx_tiles = ctx_len // tm
    alpha = float((2 * depth) ** 0.25)
    n_chunks = l // DN_CHUNK

    mod_rows = -(-(b + 1) // 8) * 8
    cvec = jnp.concatenate([c, c_ctx[None, :], jnp.zeros((mod_rows - b - 1, d), F32)], axis=0)
    mod_all = _modulation(cvec, w_mod, b_mod)
    tabs = _rope_tables(ctx_len, seq)
    xall = jnp.concatenate([ctx, x], axis=1)

    for li in range(depth):
        with_ctx = li < depth - 1
        mod = mod_all[li, :b + 1].reshape(b + 1, 6, d)
        dn, ab, gq, gk, gv, mq, mk, mv = _inproj(
            xall, mod, _pack_w_in(w_in[li]), tabs,
            _pad_lanes(gqa_q_norm[li], LANES), _pad_lanes(gqa_k_norm[li], LANES),
            _pad_lanes(mla_q_norm[li], 2 * LANES), mla_kv_norm[li].reshape(1, LANES),
            _pack_wuq(mla_w_uq[li]), _pack_wukv(mla_w_ukv[li]), ctx_tiles)

        abt = ab[:, :, AB_LANE:AB_LANE + 16].reshape(b, n_chunks, DN_CHUNK, 16).transpose(0, 1, 3, 2)
        neg_a = -jnp.exp(dn_a_log[li].reshape(-1))
        dtb = dn_dt_bias[li].reshape(-1)
        n_u = neg_a.shape[0]
        garow = jnp.zeros((2, LANES), F32).at[0, AB_LANE:AB_LANE + n_u].set(neg_a).at[1, AB_LANE:AB_LANE + n_u].set(dtb)
        gacol = jnp.zeros((16, 2), F32).at[:n_u, 0].set(neg_a).at[:n_u, 1].set(dtb)
        dno = _deltanet(dn, ab, abt, _pack_conv(dn_conv[li]), garow, gacol,
                        dn_norm[li].reshape(1, DN_DV), ctx_len)
        gqo = _attention(gq, gk, gv, n_heads=GQA_HEADS, grp=GQA_HEADS // GQA_KV_HEADS, scale=1.0,
                         ctx_len=ctx_len, with_ctx=with_ctx, name="gqa_attention")
        mlo = _attention(mq, mk, mv, n_heads=MLA_HEADS, grp=1, scale=float((MLA_NOPE + MLA_ROPE) ** -0.5),
                         ctx_len=ctx_len, with_ctx=with_ctx, name="mla_attention")

        rw = jnp.concatenate([router_w[li], jnp.zeros((d, LANES - N_EXPERTS), F32)], axis=1).astype(BF16)
        rb = jnp.concatenate([router_b[li], jnp.full((LANES - N_EXPERTS,), -1e30, F32)]).reshape(1, LANES)
        x1, hx, idx, wt = _outproj(xall, dno, gqo, mlo, mod, w_out[li].astype(BF16),
                                   ln1_g[li].reshape(1, d), ln1_b[li].reshape(1, d), rw, rb,
                                   ctx_tiles, with_ctx, alpha)
        t = x1.shape[0]
        n_tiles = -(-TOP_K * t // MOE_TILE) + N_EXPERTS
        tile_e, n_valid, row_tok, row_dst, row_gate = _route(idx[:, :TOP_K], wt[:, :TOP_K], t, n_tiles)
        y = _moe(hx, tile_e, n_valid, row_tok, row_dst, row_gate,
                 exp_w_gate[li], exp_b_gate[li], exp_w_up[li], exp_b_up[li],
                 exp_w_down[li], exp_b_down[li], n_tiles)
        xall = _final(x1, y, mod, ln2_g[li].reshape(1, d), ln2_b[li].reshape(1, d),
                      b, ctx_tiles, with_ctx, alpha)
    return xall
```

```python
import functools

import jax
import jax.numpy as jnp
from jax import lax
from jax.experimental import pallas as pl
from jax.experimental.pallas import tpu as pltpu

F32 = jnp.float32
BF16 = jnp.bfloat16

GRID_W = 64
DN_HEADS = 4
DN_DK = 64
DN_DV = 64
DN_CHUNK = 64
GQA_HEADS = 8
GQA_KV_HEADS = 2
GQA_DIM = 64
MLA_HEADS = 4
MLA_Q_RANK = 192
MLA_KV_RANK = 128
MLA_NOPE = 64
MLA_ROPE = 32
MLA_V = 64
N_EXPERTS = 32
TOP_K = 4
SWIGLU_LIMIT = 7.0
SWIGLU_ALPHA = 1.702
ROPE_THETA = 10000.0
EPS = 1e-6

LANES = 128
HALF = 64
TOKEN_TILE = 256
MOE_TILE = 256
VMEM_LIMIT = 56 * 1024 * 1024

DN_W = 1280
DN_CONV_W = 1024
GQ_OFF = DN_W
GK_OFF = GQ_OFF + GQA_HEADS * LANES
GV_OFF = GK_OFF + GQA_KV_HEADS * LANES
CQ_OFF = GV_OFF + GQA_KV_HEADS * LANES
CKV_OFF = CQ_OFF + 2 * LANES
KR_OFF = CKV_OFF + LANES
IN_W = KR_OFF + LANES
AB_LANE = HALF


def _cparams(sem):
    return pltpu.CompilerParams(dimension_semantics=sem, vmem_limit_bytes=VMEM_LIMIT)


def _ln(x):
    mu = jnp.mean(x, axis=-1, keepdims=True)
    xc = x - mu
    var = jnp.mean(xc * xc, axis=-1, keepdims=True)
    return xc * lax.rsqrt(var + EPS)


def _silu(x):
    return x * jax.nn.sigmoid(x)


def _dot(a, b):
    return jnp.dot(a, b, preferred_element_type=F32)


def _dot_nt(a, b):
    return lax.dot_general(a, b, (((1,), (1,)), ((), ())), preferred_element_type=F32)


def _dot_tn(a, b):
    return lax.dot_general(a, b, (((0,), (0,)), ((), ())), preferred_element_type=F32)


def _split_bf16(x, parts):
    out = []
    r = x
    for i in range(parts):
        p = r.astype(BF16)
        out.append(p)
        if i + 1 < parts:
            r = r - p.astype(F32)
    return out


def _mod_kernel(c_ref, w_ref, b_ref, o_ref):
    s = _silu(c_ref[...]).astype(BF16)
    o_ref[0] = _dot(s, w_ref[0].astype(BF16)) + b_ref[0]


def _modulation(cvec, w_mod, b_mod):
    depth, d, n = w_mod.shape
    rows = cvec.shape[0]
    tn = 1536
    return pl.pallas_call(
        _mod_kernel,
        grid=(depth, n // tn),
        in_specs=[
            pl.BlockSpec((rows, d), lambda l, j: (0, 0)),
            pl.BlockSpec((1, d, tn), lambda l, j: (l, 0, j)),
            pl.BlockSpec((1, 1, tn), lambda l, j: (l, 0, j)),
        ],
        out_specs=pl.BlockSpec((1, rows, tn), lambda l, j: (l, 0, j)),
        out_shape=jax.ShapeDtypeStruct((depth, rows, n), F32),
        compiler_params=_cparams(("parallel", "parallel")),
        name="modulation",
    )(cvec, w_mod, b_mod.reshape(depth, 1, n))


def _rms_slab(xs, n, w):
    ms = jnp.sum(xs * xs, axis=-1, keepdims=True) * (1.0 / n)
    return xs * lax.rsqrt(ms + EPS) * w


def _rope(y, c, s_next, s_prev, h):
    return y * c + pltpu.roll(y, LANES - h, 1) * s_next + pltpu.roll(y, h, 1) * s_prev


def _inproj_kernel(x_ref, mod_ref, w_ref, tab_ref, gqn_ref, gkn_ref, mqn_ref, mkvn_ref,
                   wuq_ref, wukv_ref,
                   dn_ref, ab_ref, gq_ref, gk_ref, gv_ref, mq_ref, mk_ref, mv_ref, abt_ref):
    x = x_ref[0]
    shift = mod_ref[0, 0:1, :]
    scale = mod_ref[0, 1:2, :]
    h = (_ln(x) * (1.0 + scale) + shift).astype(BF16)
    z = _dot(h, w_ref[...])

    dn_ref[0] = z[:, 0:DN_W]
    ab = z[:, CQ_OFF + LANES:CQ_OFF + 2 * LANES]
    ab_ref[0] = ab
    ab_t = ab.T
    for cc in range(abt_ref.shape[1]):
        abt_ref[0, cc] = ab_t[AB_LANE:AB_LANE + 16, cc * DN_CHUNK:(cc + 1) * DN_CHUNK]

    g_c, g_sn, g_sp = tab_ref[0], tab_ref[1], tab_ref[2]
    m_c, m_sn, m_sp = tab_ref[3], tab_ref[4], tab_ref[5]

    gqn = gqn_ref[...]
    for hh in range(GQA_HEADS):
        xs = z[:, GQ_OFF + hh * LANES:GQ_OFF + (hh + 1) * LANES]
        y = _rope(_rms_slab(xs, GQA_DIM, gqn), g_c, g_sn, g_sp, GQA_DIM // 4)
        gq_ref[0, :, hh * LANES:(hh + 1) * LANES] = (y * (GQA_DIM ** -0.5)).astype(BF16)
    gkn = gkn_ref[...]
    for g in range(GQA_KV_HEADS):
        xs = z[:, GK_OFF + g * LANES:GK_OFF + (g + 1) * LANES]
        y = _rope(_rms_slab(xs, GQA_DIM, gkn), g_c, g_sn, g_sp, GQA_DIM // 4)
        gk_ref[0, :, g * LANES:(g + 1) * LANES] = y.astype(BF16)
    gv_ref[0] = z[:, GV_OFF:GV_OFF + GQA_KV_HEADS * LANES].astype(BF16)

    cq = z[:, CQ_OFF:CQ_OFF + 2 * LANES]
    lane = lax.broadcasted_iota(jnp.int32, cq.shape, 1)
    cq = jnp.where(lane < MLA_Q_RANK, cq, 0.0)
    cqn = _rms_slab(cq, MLA_Q_RANK, mqn_ref[...]).astype(BF16)
    q = _dot(cqn, wuq_ref[...])
    for hh in range(MLA_HEADS):
        y = _rope(q[:, hh * LANES:(hh + 1) * LANES], m_c, m_sn, m_sp, MLA_ROPE // 4)
        mq_ref[0, :, hh * LANES:(hh + 1) * LANES] = y.astype(BF16)

    ckv = z[:, CKV_OFF:CKV_OFF + LANES]
    ckvn = _rms_slab(ckv, MLA_KV_RANK, mkvn_ref[...]).astype(BF16)
    kv = _dot(ckvn, wukv_ref[...])
    kr = _rope(z[:, KR_OFF:KR_OFF + LANES], m_c, m_sn, m_sp, MLA_ROPE // 4)
    for hh in range(MLA_HEADS):
        mk_ref[0, :, hh * LANES:(hh + 1) * LANES] = (kv[:, hh * LANES:(hh + 1) * LANES] + kr).astype(BF16)
    mv_ref[0] = kv[:, MLA_HEADS * LANES:].astype(BF16)


def _inproj(xall, mod, w_in_p, tabs, gqn, gkn, mqn, mkvn, wuq_p, wukv_p, ctx_tiles):
    b, l, d = xall.shape
    tm = TOKEN_TILE
    nt = l // tm
    n_ctx_row = mod.shape[0] - 1

    def mod_idx(bi, j):
        return (jnp.where(j < ctx_tiles, n_ctx_row, bi), 0, 0)

    row = lambda bi, j: (bi, j, 0)
    const2 = lambda bi, j: (0, 0)
    widths = (DN_W, LANES, GQA_HEADS * LANES, GQA_KV_HEADS * LANES, GQA_KV_HEADS * LANES,
              MLA_HEADS * LANES, MLA_HEADS * LANES, MLA_HEADS * LANES)
    dtypes = (F32, F32, BF16, BF16, BF16, BF16, BF16, BF16)
    return pl.pallas_call(
        _inproj_kernel,
        grid=(b, nt),
        in_specs=[
            pl.BlockSpec((1, tm, d), row),
            pl.BlockSpec((1, 6, d), mod_idx),
            pl.BlockSpec((d, IN_W), const2),
            pl.BlockSpec((6, tm, LANES), lambda bi, j: (0, j, 0)),
            pl.BlockSpec((1, LANES), const2),
            pl.BlockSpec((1, LANES), const2),
            pl.BlockSpec((1, 2 * LANES), const2),
            pl.BlockSpec((1, LANES), const2),
            pl.BlockSpec((2 * LANES, MLA_HEADS * LANES), const2),
            pl.BlockSpec((LANES, 2 * MLA_HEADS * LANES), const2),
        ],
        out_specs=[pl.BlockSpec((1, tm, w), row) for w in widths]
        + [pl.BlockSpec((1, tm // DN_CHUNK, 16, DN_CHUNK), lambda bi, j: (bi, j, 0, 0))],
        out_shape=[jax.ShapeDtypeStruct((b, l, w), dt) for w, dt in zip(widths, dtypes)]
        + [jax.ShapeDtypeStruct((b, l // DN_CHUNK, 16, DN_CHUNK), F32)],
        compiler_params=_cparams(("parallel", "parallel")),
        name="inproj",
    )(xall, mod, w_in_p, tabs, gqn, gkn, mqn, mkvn, wuq_p, wukv_p)


def _attn_kernel(q_ref, k_ref, v_ref, o_ref, *, n_heads, grp, scale, ctx_len, kv_len, with_ctx):
    def run(lk):
        outs = []
        for h in range(n_heads):
            g = h // grp
            q = q_ref[0, :, h * LANES:(h + 1) * LANES]
            k = k_ref[0, 0:lk, g * LANES:(g + 1) * LANES]
            v = v_ref[0, 0:lk, g * LANES:(g + 1) * LANES]
            s = _dot_nt(q, k)
            if scale != 1.0:
                s = s * scale
            m = jnp.max(s, axis=-1, keepdims=True)
            p = jnp.exp(s - m)
            den = jnp.sum(p, axis=-1, keepdims=True)
            outs.append(_dot(p.astype(BF16), v) / den)
        lane = lax.broadcasted_iota(jnp.int32, outs[0].shape, 1)
        for j in range(n_heads // 2):
            o_ref[0, :, j * LANES:(j + 1) * LANES] = jnp.where(
                lane < HALF, outs[2 * j], outs[2 * j + 1]).astype(o_ref.dtype)

    if with_ctx:
        j = pl.program_id(1)
        n_ctx_tiles = ctx_len // TOKEN_TILE

        @pl.when(j < n_ctx_tiles)
        def _():
            run(ctx_len)

        @pl.when(j >= n_ctx_tiles)
        def _():
            run(kv_len)
    else:
        run(kv_len)


def _attention(q, k, v, *, n_heads, grp, scale, ctx_len, with_ctx, name):
    b, l, _ = q.shape
    tq = TOKEN_TILE
    ctx_tiles = ctx_len // tq
    off = 0 if with_ctx else ctx_tiles
    kw = k.shape[-1]
    ow = n_heads * HALF
    kern = functools.partial(_attn_kernel, n_heads=n_heads, grp=grp, scale=scale,
                             ctx_len=ctx_len, kv_len=l, with_ctx=with_ctx)
    return pl.pallas_call(
        kern,
        grid=(b, l // tq - off),
        in_specs=[
            pl.BlockSpec((1, tq, n_heads * LANES), lambda bi, j: (bi, j + off, 0)),
            pl.BlockSpec((1, l, kw), lambda bi, j: (bi, 0, 0)),
            pl.BlockSpec((1, l, kw), lambda bi, j: (bi, 0, 0)),
        ],
        out_specs=pl.BlockSpec((1, tq, ow), lambda bi, j: (bi, j, 0)),
        out_shape=jax.ShapeDtypeStruct((b, l - off * tq, ow), BF16),
        compiler_params=_cparams(("parallel", "parallel")),
        name=name,
    )(q, k, v)


def _softplus(x):
    return jnp.maximum(x, 0.0) + jnp.log(1.0 + jnp.exp(-jnp.abs(x)))


def _deltanet_kernel(dn_ref, ab_ref, abt_ref, cw_ref, garow_ref, gacol_ref, nw_ref, o_ref,
                     s_scr, of_scr, ob_scr, *, seq_len, ctx_len):
    cs = DN_CHUNK
    hc = DN_HEADS * cs
    n_chunks = seq_len // cs
    nc_ctx = ctx_len // cs
    n_units = 2 * DN_HEADS
    assert cs == HALF

    ti = lax.broadcasted_iota(jnp.int32, (2 * cs, cs), 0)
    tj = lax.broadcasted_iota(jnp.int32, (2 * cs, cs), 1)
    tri_lhs = jnp.where(ti < cs, jnp.where(tj <= ti, 1.0, 0.0),
                        jnp.where(tj >= ti - cs, 1.0, 0.0)).astype(BF16)
    ui = lax.broadcasted_iota(jnp.int32, (cs, 2 * cs), 0)
    uj = lax.broadcasted_iota(jnp.int32, (cs, 2 * cs), 1)
    tri_rhs = jnp.where(uj < cs, jnp.where(uj >= ui, 1.0, 0.0),
                        jnp.where(uj - cs <= ui, 1.0, 0.0)).astype(BF16)
    li = lax.broadcasted_iota(jnp.int32, (LANES, LANES), 0)
    lj = lax.broadcasted_iota(jnp.int32, (LANES, LANES), 1)
    blockdiag = ((li >= HALF) == (lj >= HALF)).astype(BF16)
    upper_st = lax.broadcasted_iota(jnp.int32, (hc, LANES), 1) >= HALF
    rowid = lax.broadcasted_iota(jnp.int32, (cs, DN_CONV_W), 0)
    bi = lax.broadcasted_iota(jnp.int32, (hc, hc), 0)
    bj = lax.broadcasted_iota(jnp.int32, (hc, hc), 1)
    same_head = lax.shift_right_logical(bi, 6) == lax.shift_right_logical(bj, 6)

    cw0 = cw_ref[0:1, :]
    cw1 = cw_ref[1:2, :]
    cw2 = cw_ref[2:3, :]
    neg_a_row = garow_ref[0:1, :]
    dtb_row = garow_ref[1:2, :]
    neg_a_col = gacol_ref[:, 0:1]
    dtb_col = gacol_ref[:, 1:2]

    s_scr[...] = jnp.zeros(s_scr.shape, F32)

    def chunk_prep(cidx):
        r0 = pl.multiple_of(cidx * cs, cs)
        cur = dn_ref[0, pl.ds(r0, cs), 0:DN_CONV_W]
        prow = dn_ref[0, pl.ds(jnp.maximum(r0 - 1, 0), 1), 0:DN_CONV_W]
        nrow = dn_ref[0, pl.ds(jnp.minimum(r0 + cs, seq_len - 1), 1), 0:DN_CONV_W]
        has_prev = jnp.logical_and(cidx != 0, cidx != nc_ctx)
        has_next = jnp.logical_and(cidx != nc_ctx - 1, cidx != n_chunks - 1)
        prow = jnp.where(has_prev, prow, 0.0)
        nrow = jnp.where(has_next, nrow, 0.0)
        xm = jnp.where(rowid == 0, prow, pltpu.roll(cur, 1, 0))
        xp = jnp.where(rowid == cs - 1, nrow, pltpu.roll(cur, cs - 1, 0))
        y = _silu(cw0 * xm + cw1 * cur + cw2 * xp)
        a_st = jnp.concatenate([y[:, h * LANES:(h + 1) * LANES] for h in range(DN_HEADS)], axis=0)
        b_st = jnp.concatenate([y[:, (DN_HEADS + h) * LANES:(DN_HEADS + h + 1) * LANES]
                                for h in range(DN_HEADS)], axis=0)
        ab_st = jnp.concatenate([a_st, b_st], axis=0)
        inv = lax.rsqrt(_dot((ab_st * ab_st).astype(BF16), blockdiag) + EPS)
        a_st = a_st * jnp.where(upper_st, inv[0:hc], 1.0)
        b_st = b_st * inv[hc:2 * hc] * (DN_DK ** -0.5)
        ab = ab_ref[0, pl.ds(r0, cs), :]
        gslab = neg_a_row * _softplus(ab + dtb_row)
        beta = jax.nn.sigmoid(ab)
        gsum = _dot(tri_lhs, jnp.concatenate(_split_bf16(gslab, 3), axis=1))
        gsum = gsum[:, 0:LANES] + gsum[:, LANES:2 * LANES] + gsum[:, 2 * LANES:3 * LANES]
        abt = abt_ref[0, cidx]
        gt = neg_a_col * _softplus(abt + dtb_col)
        gtsum = _dot(jnp.concatenate(_split_bf16(gt, 3), axis=0), tri_rhs)
        gtsum = gtsum[0:16] + gtsum[16:32] + gtsum[32:48]
        return a_st, b_st, beta, gsum, gtsum

    def direction(d, prep):
        a_st, b_st, beta, gsum, gtsum = prep
        gc = gsum[d * cs:(d + 1) * cs]
        gct = gtsum[:, d * cs:(d + 1) * cs]
        cols = [d * DN_HEADS + h for h in range(DN_HEADS)]
        gcol = jnp.concatenate([gc[:, AB_LANE + c:AB_LANE + c + 1] for c in cols], axis=0)
        bcol = jnp.concatenate([beta[:, AB_LANE + n_units + c:AB_LANE + n_units + c + 1] for c in cols], axis=0)
        grow = jnp.concatenate([gct[c:c + 1, :] for c in cols], axis=1)
        last = cs - 1 if d == 0 else 0
        glast = jnp.concatenate([jnp.broadcast_to(gct[c:c + 1, last:last + 1], (cs, 1)) for c in cols], axis=0)
        tri = (bj <= bi) if d == 0 else (bj >= bi)
        mask = jnp.logical_and(same_head, tri)
        strict = jnp.logical_and(mask, bj != bi)
        decay = jnp.where(mask, jnp.exp(jnp.where(mask, gcol - grow, 0.0)), 0.0)
        egc = jnp.exp(gcol)
        a_k = jnp.where(upper_st, a_st, 0.0)
        b_q = jnp.where(upper_st, b_st, 0.0)
        gram = _dot_nt(jnp.concatenate([b_q, a_k], axis=0).astype(BF16), a_k.astype(BF16))
        a_qk = gram[0:hc] * decay
        nm = -jnp.where(strict, gram[hc:2 * hc] * bcol * decay, 0.0)
        x = a_st * jnp.where(upper_st, bcol * egc, bcol)
        for i in range(6):
            nb = nm.astype(BF16)
            x = x + _dot(nb, x.astype(BF16))
            if i < 5:
                nm = _dot(nb, nb)
        place = lambda z: jnp.where(same_head, jnp.concatenate([z, z], axis=1), 0.0)
        dup_hi = lambda z: jnp.where(upper_st, z, pltpu.roll(z, HALF, 1))
        s_old = s_scr[d]
        lhs = jnp.concatenate([place(dup_hi(x)), place(b_st * egc)], axis=0).astype(BF16)
        ws_qs = _dot(lhs, s_old.astype(BF16))
        v_new = x[:, 0:DN_DV] - ws_qs[0:hc]
        vb = v_new.astype(BF16)
        o = ws_qs[hc:2 * hc] + _dot(a_qk.astype(BF16), vb)
        k_tail = (place(dup_hi(a_st)) * jnp.exp(glast - gcol)).astype(BF16)
        s_scr[d] = s_old * jnp.exp(glast) + _dot_tn(k_tail, vb)
        return o

    def body(n, carry):
        cf = n
        cb = jnp.where(n < nc_ctx, nc_ctx - 1 - n, n_chunks + nc_ctx - 1 - n)
        o_f = direction(0, chunk_prep(cf))
        o_b = direction(1, chunk_prep(cb))
        rf = pl.multiple_of(cf * cs, cs)
        rb = pl.multiple_of(cb * cs, cs)
        for h in range(DN_HEADS):
            of_scr[h, pl.ds(rf, cs), :] = o_f[h * cs:(h + 1) * cs]
            ob_scr[h, pl.ds(rb, cs), :] = o_b[h * cs:(h + 1) * cs]
        return carry

    lax.fori_loop(0, n_chunks, body, 0)

    nw = nw_ref[...]
    rt = TOKEN_TILE

    def out_body(t, carry):
        r0 = pl.multiple_of(t * rt, rt)
        gate = dn_ref[0, pl.ds(r0, rt), DN_CONV_W:DN_W]
        sg = _silu(gate)
        ys = []
        for h in range(DN_HEADS):
            o = of_scr[h, pl.ds(r0, rt), :] + ob_scr[h, pl.ds(r0, rt), :]
            ms = jnp.mean(o * o, axis=-1, keepdims=True)
            ys.append(o * lax.rsqrt(ms + EPS) * nw * sg[:, h * DN_DV:(h + 1) * DN_DV])
        o_ref[0, pl.ds(r0, rt), :] = jnp.concatenate(ys, axis=-1).astype(o_ref.dtype)
        return carry

    lax.fori_loop(0, seq_len // rt, out_body, 0)


def _deltanet(dn, ab, abt, cw, garow, gacol, nw, ctx_len):
    b, l, _ = dn.shape
    n_chunks = l // DN_CHUNK
    kern = functools.partial(_deltanet_kernel, seq_len=l, ctx_len=ctx_len)
    return pl.pallas_call(
        kern,
        grid=(b,),
        in_specs=[
            pl.BlockSpec((1, l, DN_W), lambda bi: (bi, 0, 0)),
            pl.BlockSpec((1, l, LANES), lambda bi: (bi, 0, 0)),
            pl.BlockSpec((1, n_chunks, 16, DN_CHUNK), lambda bi: (bi, 0, 0, 0)),
            pl.BlockSpec((3, DN_CONV_W), lambda bi: (0, 0)),
            pl.BlockSpec((2, LANES), lambda bi: (0, 0)),
            pl.BlockSpec((16, 2), lambda bi: (0, 0)),
            pl.BlockSpec((1, DN_DV), lambda bi: (0, 0)),
        ],
        out_specs=pl.BlockSpec((1, l, DN_HEADS * DN_DV), lambda bi: (bi, 0, 0)),
        out_shape=jax.ShapeDtypeStruct((b, l, DN_HEADS * DN_DV), BF16),
        scratch_shapes=[
            pltpu.VMEM((2, DN_HEADS * DN_DK, DN_DV), F32),
            pltpu.VMEM((DN_HEADS, l, DN_DV), F32),
            pltpu.VMEM((DN_HEADS, l, DN_DV), F32),
        ],
        compiler_params=_cparams(("parallel",)),
        name="deltanet",
    )(dn, ab, abt, cw, garow, gacol, nw)


def _outproj_kernel(x_ref, dno_ref, gqo_ref, mlo_ref, mod_ref, w_ref, g_ref, b_ref, rw_ref, rb_ref,
                    x1_ref, hx_ref, rt_ref, cnt_ref, *, alpha):
    mix = jnp.concatenate([dno_ref[0], gqo_ref[0], mlo_ref[0]], axis=-1)
    o = _dot(mix, w_ref[...])
    gate1 = mod_ref[0, 2:3, :]
    x1 = _ln(alpha * x_ref[0] + gate1 * o) * g_ref[...] + b_ref[...]
    x1_ref[...] = x1
    hx = _ln(x1) * (1.0 + mod_ref[0, 4:5, :]) + mod_ref[0, 3:4, :]
    for jj in range(hx_ref.shape[1]):
        hx_ref[:, jj, :] = hx[:, jj * LANES:(jj + 1) * LANES]
    logits = _dot(hx.astype(BF16), rw_ref[...]) + rb_ref[...]
    lane = lax.broadcasted_iota(jnp.int32, logits.shape, 1)
    vals, idxs = [], []
    cur = logits
    for _ in range(TOP_K):
        m = jnp.max(cur, axis=-1, keepdims=True)
        idx = jnp.min(jnp.where(cur == m, lane, LANES), axis=-1, keepdims=True)
        vals.append(m)
        idxs.append(idx)
        cur = jnp.where(lane == idx, -jnp.inf, cur)
    es = [jnp.exp(v - vals[0]) for v in vals]
    den = es[0] + es[1] + es[2] + es[3]
    slab = jnp.zeros(logits.shape, F32)
    hist = jnp.zeros((1, LANES), F32)
    for kk in range(TOP_K):
        slab = jnp.where(lane == kk, idxs[kk].astype(F32), slab)
        slab = jnp.where(lane == 8 + kk, es[kk] / den, slab)
        hist = hist + jnp.sum((lane == idxs[kk]).astype(F32), axis=0, keepdims=True)
    rt_ref[...] = slab.T[0:16, :]
    cnt_ref[0] = jnp.broadcast_to(hist, (8, LANES))


def _outproj(xall, dno, gqo, mlo, mod, w_out, ln_g, ln_b, rw, rb, ctx_tiles, with_ctx, alpha):
    b, l, d = xall.shape
    tm = TOKEN_TILE
    nt = l // tm
    off = 0 if with_ctx else ctx_tiles
    ntl = nt - off
    n_ctx_row = mod.shape[0] - 1

    def mod_idx(bi, j):
        return (jnp.where(j + off < ctx_tiles, n_ctx_row, bi), 0, 0)

    row = lambda bi, j: (bi, j + off, 0)
    qrow = lambda bi, j: (bi, j, 0)
    flat = lambda bi, j: (bi * ntl + j, 0)
    const2 = lambda bi, j: (0, 0)
    t = b * ntl * tm
    return pl.pallas_call(
        functools.partial(_outproj_kernel, alpha=alpha),
        grid=(b, ntl),
        in_specs=[
            pl.BlockSpec((1, tm, d), row),
            pl.BlockSpec((1, tm, dno.shape[-1]), row),
            pl.BlockSpec((1, tm, gqo.shape[-1]), qrow),
            pl.BlockSpec((1, tm, mlo.shape[-1]), qrow),
            pl.BlockSpec((1, 6, d), mod_idx),
            pl.BlockSpec((d, d), const2),
            pl.BlockSpec((1, d), const2),
            pl.BlockSpec((1, d), const2),
            pl.BlockSpec((d, LANES), const2),
            pl.BlockSpec((1, LANES), const2),
        ],
        out_specs=[
            pl.BlockSpec((tm, d), flat),
            pl.BlockSpec((tm, d // LANES, LANES), lambda bi, j: (bi * ntl + j, 0, 0)),
            pl.BlockSpec((16, tm), lambda bi, j: (0, bi * ntl + j)),
            pl.BlockSpec((1, 8, LANES), lambda bi, j: (bi * ntl + j, 0, 0)),
        ],
        out_shape=[
            jax.ShapeDtypeStruct((t, d), F32),
            jax.ShapeDtypeStruct((t, d // LANES, LANES), F32),
            jax.ShapeDtypeStruct((16, t), F32),
            jax.ShapeDtypeStruct((t // tm, 8, LANES), F32),
        ],
        compiler_params=_cparams(("parallel", "parallel")),
        name="outproj_router",
    )(xall, dno, gqo, mlo, mod, w_out, ln_g, ln_b, rw, rb)


def _moe_kernel(te_ref, nv_ref,
                tok_ref, tokn_ref, dst_ref,
                hx_hbm, gate_ref, wg_ref, bg_ref, wu_ref, bu_ref, wd_ref, bd_ref,
                y_hbm,
                xbuf, obuf, wg_b, wu_b, wd_b, gsem, ssem):
    i = pl.program_id(0)
    n_valid = nv_ref[0]
    tm = MOE_TILE
    slot = lax.rem(i, 2)

    nsub = xbuf.shape[2]

    def start_gather(idx_ref, s):
        def row(r, c):
            pltpu.make_async_copy(hx_hbm.at[idx_ref[0, 0, r]], xbuf.at[s, r], gsem.at[s]).start()
            return c
        lax.fori_loop(0, tm, row, 0, unroll=8)

    def wait_gather(s):
        pltpu.make_async_copy(xbuf.at[s], xbuf.at[s], gsem.at[s]).wait()

    def start_scatter(s):
        def row(r, c):
            pltpu.make_async_copy(obuf.at[s, r], y_hbm.at[dst_ref[0, 0, r]], ssem.at[s]).start()
            return c
        lax.fori_loop(0, tm, row, 0, unroll=8)

    def wait_scatter(s):
        pltpu.make_async_copy(obuf.at[s], obuf.at[s], ssem.at[s]).wait()

    @pl.when(i == 0)
    def _():
        start_gather(tok_ref, 0)
        n_real = y_hbm.shape[0] - 2 * tm
        obuf[...] = jnp.zeros(obuf.shape, F32)
        for s in range(2):
            spare = pltpu.make_async_copy(obuf.at[s], y_hbm.at[pl.ds(n_real + s * tm, tm)], ssem.at[s])
            spare.start()
            spare.wait()

    @pl.when(i + 1 < n_valid)
    def _():
        start_gather(tokn_ref, 1 - slot)

    @pl.when(i < n_valid)
    def _():
        prev_e = te_ref[jnp.maximum(i - 1, 0)]
        new_expert = jnp.logical_or(i == 0, te_ref[i] != prev_e)

        @pl.when(new_expert)
        def _():
            wg_b[...] = wg_ref[0, 0].astype(BF16)
            wu_b[...] = wu_ref[0, 0].astype(BF16)
            wd_b[...] = wd_ref[0, 0].astype(BF16)

        wait_gather(slot)

        @pl.when(i >= 2)
        def _():
            wait_scatter(slot)

        xb = jnp.concatenate([xbuf[slot, :, jj, :] for jj in range(nsub)], axis=-1).astype(BF16)
        gl = jnp.minimum(_dot(xb, wg_b[...]) + bg_ref[0, 0], SWIGLU_LIMIT)
        up = jnp.clip(_dot(xb, wu_b[...]) + bu_ref[0, 0], -SWIGLU_LIMIT, SWIGLU_LIMIT)
        act = gl * jax.nn.sigmoid(SWIGLU_ALPHA * gl) * (up + 1.0)
        out = (_dot(act.astype(BF16), wd_b[...]) + bd_ref[0, 0]) * gate_ref[...]
        for jj in range(nsub):
            obuf[slot, :, jj, :] = out[:, jj * LANES:(jj + 1) * LANES]
        start_scatter(slot)

        @pl.when(i == n_valid - 1)
        def _():
            @pl.when(i >= 1)
            def _():
                wait_scatter(1 - slot)
            wait_scatter(slot)


def _moe(hx, tile_e, n_valid, row_tok, row_dst, row_gate, wg, bg, wu, bu, wd, bd, n_tiles, li):
    t, nsub, _ = hx.shape
    depth, e, d, de = wg.shape
    tm = MOE_TILE
    wspec = lambda shp: pl.BlockSpec((1,) + shp, lambda i, te, nv: (li, te[i], 0, 0))
    ispec = lambda nxt: pl.BlockSpec((1, 1, tm), lambda i, te, nv: (jnp.minimum(i + nxt, n_tiles - 1), 0, 0),
                                     memory_space=pltpu.SMEM)
    grid_spec = pltpu.PrefetchScalarGridSpec(
        num_scalar_prefetch=2,
        grid=(n_tiles,),
        in_specs=[
            ispec(0), ispec(1), ispec(0),
            pl.BlockSpec(memory_space=pl.ANY),
            pl.BlockSpec((tm, 1), lambda i, te, nv: (i, 0)),
            wspec((1, d, de)), wspec((1, 1, de)),
            wspec((1, d, de)), wspec((1, 1, de)),
            wspec((1, de, d)), wspec((1, 1, d)),
        ],
        out_specs=pl.BlockSpec(memory_space=pl.ANY),
        scratch_shapes=[
            pltpu.VMEM((2, tm, nsub, LANES), F32),
            pltpu.VMEM((2, tm, nsub, LANES), F32),
            pltpu.VMEM((d, de), BF16),
            pltpu.VMEM((d, de), BF16),
            pltpu.VMEM((de, d), BF16),
            pltpu.SemaphoreType.DMA((2,)),
            pltpu.SemaphoreType.DMA((2,)),
        ],
    )
    return pl.pallas_call(
        _moe_kernel,
        grid_spec=grid_spec,
        out_shape=jax.ShapeDtypeStruct((TOP_K * t + 2 * tm, nsub, LANES), F32),
        compiler_params=_cparams(("arbitrary",)),
        name="moe_experts",
    )(tile_e, n_valid, row_tok, row_tok, row_dst, hx, row_gate,
      wg, bg.reshape(depth, e, 1, de), wu, bu.reshape(depth, e, 1, de), wd, bd.reshape(depth, e, 1, d))


def _route(rt, cnt, t, n_tiles):
    tm = MOE_TILE
    nf = TOP_K * t
    i32 = jnp.int32
    flat_e = rt[0:TOP_K].astype(i32).reshape(nf)
    flat_w = rt[8:8 + TOP_K].reshape(nf)
    skey, sgate = lax.sort((flat_e * nf + jnp.arange(nf, dtype=i32), flat_w), num_keys=1)
    order = skey - (skey // nf) * nf
    counts = jnp.sum(cnt[:, 0, :N_EXPERTS], axis=0).astype(i32)
    tiles_per_e = (counts + tm - 1) // tm
    tile_end = jnp.cumsum(tiles_per_e)
    tile_off = tile_end - tiles_per_e
    real_off = jnp.cumsum(counts) - counts
    n_valid = tile_end[-1]
    tile_id = jnp.arange(n_tiles, dtype=i32)
    tile_e = jnp.sum((tile_id[:, None] >= tile_end[None, :]).astype(i32), axis=1)
    last_e = jnp.sum((n_valid - 1 >= tile_end).astype(i32))
    tile_e = jnp.where(tile_id < n_valid, tile_e, last_e).astype(i32)
    onehot = (tile_e[:, None] == jnp.arange(N_EXPERTS, dtype=i32)[None, :]).astype(i32)
    pick = lambda v: jnp.sum(onehot * v[None, :], axis=1)
    rank0 = (tile_id - pick(tile_off)) * tm
    n_rows = jnp.where(tile_id < n_valid, jnp.clip(pick(counts) - rank0, 0, tm), 0)
    start = jnp.clip(pick(real_off) + rank0, 0, nf)
    window = lambda v: jax.vmap(lambda s: lax.dynamic_slice(v, (s,), (tm,)))(start)
    f = window(jnp.concatenate([order, jnp.zeros((tm,), i32)]))
    g = window(jnp.concatenate([sgate, jnp.zeros((tm,), F32)]))
    r = jnp.arange(tm, dtype=i32)[None, :]
    valid = r < n_rows[:, None]
    spare = nf + (tile_id % 2)[:, None] * tm + r
    tok = jnp.where(valid, f - (f // t) * t, 0)
    dst = jnp.where(valid, f, spare)
    gate = jnp.where(valid, g, 0.0)
    return (tile_e, n_valid.reshape(1).astype(i32), tok.reshape(n_tiles, 1, tm).astype(i32),
            dst.reshape(n_tiles, 1, tm).astype(i32), gate.reshape(n_tiles * tm, 1).astype(F32))


def _final_kernel(x1_ref, y0_ref, y1_ref, y2_ref, y3_ref, mod_ref, g_ref, b_ref, o_ref, *, alpha):
    rows = lambda ref: jnp.concatenate([ref[:, jj, :] for jj in range(ref.shape[1])], axis=-1)
    y = (rows(y0_ref) + rows(y1_ref)) + (rows(y2_ref) + rows(y3_ref))
    gate2 = mod_ref[0, 5:6, :]
    o_ref[0] = _ln(alpha * x1_ref[...] + gate2 * y) * g_ref[...] + b_ref[...]


def _final(x1, y, mod, ln_g, ln_b, b, ctx_tiles, with_ctx, alpha):
    t, d = x1.shape
    tm = TOKEN_TILE
    ntl = t // (b * tm)
    tt = t // tm
    off = 0 if with_ctx else ctx_tiles
    n_ctx_row = mod.shape[0] - 1

    def mod_idx(bi, j):
        return (jnp.where(j + off < ctx_tiles, n_ctx_row, bi), 0, 0)

    yspec = lambda kk: pl.BlockSpec((tm, d // LANES, LANES), lambda bi, j: (kk * tt + bi * ntl + j, 0, 0))
    return pl.pallas_call(
        functools.partial(_final_kernel, alpha=alpha),
        grid=(b, ntl),
        in_specs=[
            pl.BlockSpec((tm, d), lambda bi, j: (bi * ntl + j, 0)),
            yspec(0), yspec(1), yspec(2), yspec(3),
            pl.BlockSpec((1, 6, d), mod_idx),
            pl.BlockSpec((1, d), lambda bi, j: (0, 0)),
            pl.BlockSpec((1, d), lambda bi, j: (0, 0)),
        ],
        out_specs=pl.BlockSpec((1, tm, d), lambda bi, j: (bi, j, 0)),
        out_shape=jax.ShapeDtypeStruct((b, ntl * tm, d), F32),
        compiler_params=_cparams(("parallel", "parallel")),
        name="moe_combine_ln",
    )(x1, y, y, y, y, mod, ln_g, ln_b)


def _pack_w_in(w):
    d = w.shape[0]
    z = lambda n: jnp.zeros((d, n), w.dtype)
    qk_w = DN_HEADS * DN_DK
    o_q, o_k, o_v = 0, qk_w, 2 * qk_w
    o_gate = 2 * qk_w + DN_HEADS * DN_DV
    o_ab = o_gate + DN_HEADS * DN_DV
    o_gq = o_ab + 4 * DN_HEADS
    o_gk = o_gq + GQA_HEADS * GQA_DIM
    o_gv = o_gk + GQA_KV_HEADS * GQA_DIM
    o_cq = o_gv + GQA_KV_HEADS * GQA_DIM
    o_ckv = o_cq + MLA_Q_RANK
    o_kr = o_ckv + MLA_KV_RANK
    col = lambda o, i, n: w[:, o + i * n:o + (i + 1) * n]
    parts = []
    for h in range(DN_HEADS):
        parts += [col(o_v, h, DN_DV), col(o_k, h, DN_DK)]
    for h in range(DN_HEADS):
        parts += [col(o_q, h, DN_DK), col(o_q, h, DN_DK)]
    parts.append(w[:, o_gate:o_gate + DN_HEADS * DN_DV])
    for h in range(GQA_HEADS):
        parts += [col(o_gq, h, GQA_DIM), z(HALF)]
    for g in range(GQA_KV_HEADS):
        parts += [col(o_gk, g, GQA_DIM), z(HALF)]
    for g in range(GQA_KV_HEADS):
        parts += [col(o_gv, g, GQA_DIM), col(o_gv, g, GQA_DIM)]
    parts += [w[:, o_cq:o_cq + MLA_Q_RANK], w[:, o_ab:o_ab + 4 * DN_HEADS],
              z(2 * LANES - MLA_Q_RANK - 4 * DN_HEADS)]
    parts.append(w[:, o_ckv:o_ckv + MLA_KV_RANK])
    parts += [z(HALF), w[:, o_kr:o_kr + MLA_ROPE], z(LANES - HALF - MLA_ROPE)]
    out = jnp.concatenate(parts, axis=1)
    assert out.shape[1] == IN_W, out.shape
    return out.astype(BF16)


def _pack_conv(cw):
    qk_w = DN_HEADS * DN_DK
    col = lambda o, i, n: cw[:, o + i * n:o + (i + 1) * n]
    parts = []
    for h in range(DN_HEADS):
        parts += [col(2 * qk_w, h, DN_DV), col(qk_w, h, DN_DK)]
    for h in range(DN_HEADS):
        parts += [col(0, h, DN_DK), col(0, h, DN_DK)]
    return jnp.concatenate(parts, axis=1)


def _pack_wuq(w):
    qd = MLA_NOPE + MLA_ROPE
    parts = []
    for h in range(MLA_HEADS):
        parts += [w[:, h * qd:(h + 1) * qd], jnp.zeros((w.shape[0], LANES - qd), w.dtype)]
    out = jnp.concatenate(parts, axis=1)
    out = jnp.concatenate([out, jnp.zeros((2 * LANES - MLA_Q_RANK, out.shape[1]), w.dtype)], axis=0)
    return out.astype(BF16)


def _pack_wukv(w):
    kd = MLA_NOPE + MLA_V
    kparts, vparts = [], []
    for h in range(MLA_HEADS):
        kparts += [w[:, h * kd:h * kd + MLA_NOPE], jnp.zeros((w.shape[0], LANES - MLA_NOPE), w.dtype)]
        v = w[:, h * kd + MLA_NOPE:(h + 1) * kd]
        vparts += [v, v]
    return jnp.concatenate(kparts + vparts, axis=1).astype(BF16)


def _rope_tables(ctx_len, seq):
    t = jnp.arange(seq, dtype=jnp.int32)
    row = (t // GRID_W).astype(F32)[:, None]
    col = (t % GRID_W).astype(F32)[:, None]
    lane = jnp.arange(LANES, dtype=jnp.int32)[None, :]

    def tables(first, width, half):
        dd = lane - first
        active = jnp.logical_and(dd >= 0, dd < width)
        dd = jnp.clip(dd, 0, width - 1)
        use_col = dd >= width // 2
        p = dd % (width // 2)
        fi = (p % half).astype(F32)
        inv = ROPE_THETA ** (-fi / half)
        ang = jnp.where(use_col, col, row) * inv
        cos = jnp.where(active, jnp.cos(ang), 1.0)
        sin = jnp.where(active, jnp.sin(ang), 0.0)
        s_next = jnp.where(p < half, -sin, 0.0)
        s_prev = jnp.where(p >= half, sin, 0.0)
        return [cos, s_next, s_prev]

    lat = jnp.stack(tables(0, GQA_DIM, GQA_DIM // 4) + tables(HALF, MLA_ROPE, MLA_ROPE // 4))
    ident = jnp.stack([jnp.ones((ctx_len, LANES), F32), jnp.zeros((ctx_len, LANES), F32),
                       jnp.zeros((ctx_len, LANES), F32)] * 2)
    return jnp.concatenate([ident, lat], axis=1)


def _pad_lanes(v, n):
    return jnp.concatenate([v, jnp.zeros((n - v.shape[0],), v.dtype)]).reshape(1, n)


def kernel(x, c, ctx, c_ctx, w_mod, b_mod, w_in, dn_conv, dn_a_log, dn_dt_bias, dn_norm, gqa_q_norm, gqa_k_norm, mla_q_norm, mla_kv_norm, mla_w_uq, mla_w_ukv, w_out, ln1_g, ln1_b, router_w, router_b, exp_w_gate, exp_b_gate, exp_w_up, exp_b_up, exp_w_down, exp_b_down, ln2_g, ln2_b):
    b, seq, d = x.shape
    ctx_len = ctx.shape[1]
    depth = w_mod.shape[0]
    l = ctx_len + seq
    tm = TOKEN_TILE
    assert ctx_len % tm == 0 and seq % tm == 0 and seq % GRID_W == 0
    ctx_tiles = ctx_len // tm
    alpha = float((2 * depth) ** 0.25)
    n_chunks = l // DN_CHUNK

    mod_rows = -(-(b + 1) // 8) * 8
    cvec = jnp.concatenate([c, c_ctx[None, :], jnp.zeros((mod_rows - b - 1, d), F32)], axis=0)
    mod_all = _modulation(cvec, w_mod, b_mod)
    tabs = _rope_tables(ctx_len, seq)
    xall = jnp.concatenate([ctx, x], axis=1)

    for li in range(depth):
        with_ctx = li < depth - 1
        mod = mod_all[li, :b + 1].reshape(b + 1, 6, d)
        dn, ab, gq, gk, gv, mq, mk, mv, abt = _inproj(
            xall, mod, _pack_w_in(w_in[li]), tabs,
            _pad_lanes(gqa_q_norm[li], LANES), _pad_lanes(gqa_k_norm[li], LANES),
            _pad_lanes(mla_q_norm[li], 2 * LANES), mla_kv_norm[li].reshape(1, LANES),
            _pack_wuq(mla_w_uq[li]), _pack_wukv(mla_w_ukv[li]), ctx_tiles)

        neg_a = -jnp.exp(dn_a_log[li].reshape(-1))
        dtb = dn_dt_bias[li].reshape(-1)
        n_u = neg_a.shape[0]
        garow = jnp.zeros((2, LANES), F32).at[0, AB_LANE:AB_LANE + n_u].set(neg_a).at[1, AB_LANE:AB_LANE + n_u].set(dtb)
        gacol = jnp.zeros((16, 2), F32).at[:n_u, 0].set(neg_a).at[:n_u, 1].set(dtb)
        dno = _deltanet(dn, ab, abt, _pack_conv(dn_conv[li]), garow, gacol,
                        dn_norm[li].reshape(1, DN_DV), ctx_len)
        gqo = _attention(gq, gk, gv, n_heads=GQA_HEADS, grp=GQA_HEADS // GQA_KV_HEADS, scale=1.0,
                         ctx_len=ctx_len, with_ctx=with_ctx, name="gqa_attention")
        mlo = _attention(mq, mk, mv, n_heads=MLA_HEADS, grp=1, scale=float((MLA_NOPE + MLA_ROPE) ** -0.5),
                         ctx_len=ctx_len, with_ctx=with_ctx, name="mla_attention")

        rw = jnp.concatenate([router_w[li], jnp.zeros((d, LANES - N_EXPERTS), F32)], axis=1).astype(BF16)
        rb = jnp.concatenate([router_b[li], jnp.full((LANES - N_EXPERTS,), -1e30, F32)]).reshape(1, LANES)
        x1, hx, rt, cnt = _outproj(xall, dno, gqo, mlo, mod, w_out[li].astype(BF16),
                                   ln1_g[li].reshape(1, d), ln1_b[li].reshape(1, d), rw, rb,
                                   ctx_tiles, with_ctx, alpha)
        t = x1.shape[0]
        n_tiles = -(-TOP_K * t // MOE_TILE) + N_EXPERTS
        tile_e, n_valid, row_tok, row_dst, row_gate = _route(rt, cnt, t, n_tiles)
        y = _moe(hx, tile_e, n_valid, row_tok, row_dst, row_gate,
                 exp_w_gate, exp_b_gate, exp_w_up, exp_b_up, exp_w_down, exp_b_down, n_tiles, li)
        xall = _final(x1, y, mod, ln2_g[li].reshape(1, d), ln2_b[li].reshape(1, d),
                      b, ctx_tiles, with_ctx, alpha)
    return xall
```
